```python
import math
import jax, jax.numpy as jnp
from jax import lax
import numpy as np

D_MODEL = 1024
BATCH = 8
SEQ = 2048
DEPTH = 4
DEC_BATCH = 32
DEC_SEQ = 4
PAST_LEN = 8192
PAGE_SIZE = 128

N_EVEN = (DEPTH + 1) // 2
N_ODD = DEPTH // 2
MIX_HALF = D_MODEL // 2
GLA_HEADS = 4
GLA_DV = MIX_HALF // GLA_HEADS
GLA_DK = GLA_DV // 2
GLA_RANK = 16
GLA_TAU = 16.0
GLA_CHUNK = 64
FOX_HEADS = 4
FOX_DH = MIX_HALF // FOX_HEADS
FOX_BLOCK = 128
FOX_BIAS_LO = 3.0
FOX_BIAS_HI = 9.0
POOL_WINDOWS = (2, 4, 8, 16)
POOL_GROUPS = 4
POOL_GC = D_MODEL // POOL_GROUPS
POOL_BUF = 15
D_FF = 4 * D_MODEL
EPS = 1e-6
IN_SIZES = (GLA_HEADS * GLA_DK, GLA_HEADS * GLA_DK, GLA_HEADS * GLA_DV, GLA_RANK, GLA_HEADS * GLA_DV,
            FOX_HEADS * FOX_DH, FOX_HEADS * FOX_DH, FOX_HEADS * FOX_DH, FOX_HEADS)
IN_COLS = sum(IN_SIZES)

kernel_name = "gla_fox_pool_hybrid_step"


def rmsnorm(x, g):
    xf = x.astype(jnp.float32)
    y = xf * lax.rsqrt(jnp.mean(xf * xf, axis=-1, keepdims=True) + EPS)
    return (y * g.astype(jnp.float32)).astype(x.dtype)


def sq_relu_mlp(h, w_up, w_down):
    a = jax.nn.relu(h @ w_up)
    return (a * a) @ w_down


def even_project(h, w_in, w_gate_up, b_gate, b_forget):
    B, L, _ = h.shape
    splits = [int(c) for c in np.cumsum(IN_SIZES)[:-1]]
    q_a, k_a, v_a, r_a, g_a, q_b, k_b, v_b, f_b = jnp.split(h @ w_in, splits, axis=-1)
    log_alpha = jax.nn.log_sigmoid((r_a @ w_gate_up + b_gate).astype(jnp.float32)) / GLA_TAU
    logf = jax.nn.log_sigmoid((f_b + b_forget).astype(jnp.float32))
    rs = lambda t, H: t.reshape(B, L, H, -1)
    return (rs(q_a, GLA_HEADS), rs(k_a, GLA_HEADS), rs(v_a, GLA_HEADS), rs(log_alpha, GLA_HEADS), g_a,
            rs(q_b, FOX_HEADS), rs(k_b, FOX_HEADS), rs(v_b, FOX_HEADS), logf)


def gla_scan(q, k, v, log_a, S0):
    out_dtype, st_dtype = v.dtype, S0.dtype
    B, L, H, DK = q.shape
    C = GLA_CHUNK if L % GLA_CHUNK == 0 else L
    n = L // C
    f32 = jnp.float32

    def to_chunks(t):
        return t.astype(f32).reshape(B, n, C, H, t.shape[-1]).transpose(1, 0, 3, 2, 4)

    qc, kc, vc, gc = to_chunks(q * (DK ** -0.5)), to_chunks(k), to_chunks(v), to_chunks(log_a)
    tril = jnp.tril(jnp.ones((C, C), dtype=bool))[None, None, :, :, None]

    def step(S, inp):
        qi, ki, vi, gi = inp
        b = jnp.cumsum(gi, axis=2)
        o_inter = jnp.einsum('bhtk,bhkv->bhtv', qi * jnp.exp(b), S)
        diff = b[:, :, :, None, :] - b[:, :, None, :, :]
        decay = jnp.exp(jnp.where(tril, diff, -jnp.inf))
        attn = jnp.einsum('bhtk,bhsk,bhtsk->bhts', qi, ki, decay)
        o = o_inter + jnp.einsum('bhts,bhsv->bhtv', attn, vi)
        b_last = b[:, :, -1:, :]
        S_new = jnp.exp(b_last[:, :, 0, :])[..., None] * S + jnp.einsum(
            'bhsk,bhsv->bhkv', ki * jnp.exp(b_last - b), vi)
        return S_new, o

    S, o = lax.scan(step, S0.astype(f32), (qc, kc, vc, gc))
    o = o.transpose(1, 0, 3, 2, 4).reshape(B, L, H, -1)
    return o.astype(out_dtype), S.astype(st_dtype)


def fox_prompt(q, k, v, logf):
    B, L, H, D = q.shape
    Fh = jnp.cumsum(logf, axis=1).transpose(0, 2, 1)
    QB = FOX_BLOCK if L % FOX_BLOCK == 0 else L
    nb = L // QB
    qb = q.reshape(B, nb, QB, H, D).transpose(1, 0, 2, 3, 4)
    Fqb = Fh.reshape(B, H, nb, QB).transpose(2, 0, 1, 3)
    pos_k = jnp.arange(L)
    scale = D ** -0.5

    def block(args):
        i, qi, Fi = args
        s = jnp.einsum('bqhd,bkhd->bhqk', qi, k).astype(jnp.float32) * scale
        s = s + Fi[..., None] - Fh[:, :, None, :]
        pos_q = i * QB + jnp.arange(QB)
        s = jnp.where(pos_k[None, :] <= pos_q[:, None], s, -jnp.inf)
        p = jax.nn.softmax(s, axis=-1)
        return jnp.einsum('bhqk,bkhd->bqhd', p.astype(v.dtype), v)

    o = lax.map(block, (jnp.arange(nb), qb, Fqb))
    return o.transpose(1, 0, 2, 3, 4).reshape(B, L, H, D)


def fox_sample(q, k_new, v_new, logf_new, k_past, v_past, logf_past):
    B, T, H, D = q.shape
    P = k_past.shape[1]
    scale = D ** -0.5
    lp = logf_past.astype(jnp.float32)
    FnH = jnp.cumsum(logf_new.astype(jnp.float32), axis=1).transpose(0, 2, 1)
    RH = (lax.cumsum(lp, axis=1, reverse=True) - lp).transpose(0, 2, 1)
    s_past = jnp.einsum('bqhd,bkhd->bhqk', q, k_past).astype(jnp.float32) * scale
    s_past = s_past + FnH[..., None] + RH[:, :, None, :]
    s_new = jnp.einsum('bqhd,bkhd->bhqk', q, k_new).astype(jnp.float32) * scale
    s_new = s_new + FnH[..., None] - FnH[:, :, None, :]
    s_new = jnp.where(jnp.tril(jnp.ones((T, T), dtype=bool)), s_new, -jnp.inf)
    p = jax.nn.softmax(jnp.concatenate([s_past, s_new], axis=-1), axis=-1).astype(v_new.dtype)
    return (jnp.einsum('bhqk,bkhd->bqhd', p[..., :P], v_past)
            + jnp.einsum('bhqk,bkhd->bqhd', p[..., P:], v_new))


def even_merge(o_a, g_a, o_b, gnorm, w_out):
    B, L = o_a.shape[:2]
    o_a = rmsnorm(o_a, gnorm) * jax.nn.silu(g_a.reshape(B, L, GLA_HEADS, GLA_DV))
    o = jnp.concatenate([o_a.reshape(B, L, -1), o_b.reshape(B, L, -1)], axis=-1)
    return o @ w_out


def pool_mix(u, buf, pos0, w_pool, scale):
    B, L, D = u.shape
    ext = jnp.concatenate([buf.astype(u.dtype), u], axis=1)
    cs = jnp.concatenate([jnp.zeros((B, 1, D), jnp.float32),
                          jnp.cumsum(ext.astype(jnp.float32), axis=1)], axis=1)
    end = cs[:, POOL_BUF + 1:]
    pos = pos0 + jnp.arange(L) + 1
    outs = []
    for g, w in enumerate(POOL_WINDOWS):
        sl = slice(g * POOL_GC, (g + 1) * POOL_GC)
        start = cs[:, POOL_BUF + 1 - w: POOL_BUF + 1 - w + L, sl]
        cnt = jnp.minimum(w, pos).astype(jnp.float32)
        outs.append((end[..., sl] - start) / cnt[None, :, None])
    pooled = jnp.concatenate(outs, axis=-1) - u.astype(jnp.float32)
    y = jnp.einsum('blgc,gcd->blgd', pooled.reshape(B, L, POOL_GROUPS, POOL_GC).astype(u.dtype), w_pool)
    return y.reshape(B, L, D) * scale, ext[:, -POOL_BUF:]


def setup_inputs(seed: int = 0) -> dict:
    key = jax.random.key(seed)
    ks = jax.random.split(key, 24)
    n_pages = PAST_LEN // PAGE_SIZE
    n_pool = (DEC_BATCH * n_pages * 5) // 4
    nrm = lambda k, s: jax.random.normal(k, s, jnp.float32)
    page_table = jax.random.permutation(ks[0], n_pool)[:DEC_BATCH * n_pages].reshape(
        DEC_BATCH, n_pages).astype(jnp.int32)
    head_bias = jnp.linspace(FOX_BIAS_LO, FOX_BIAS_HI, FOX_HEADS).astype(jnp.float32)
    return {
        "x_prompt": nrm(ks[1], (BATCH, SEQ, D_MODEL)),
        "x_sample": nrm(ks[2], (DEC_BATCH, DEC_SEQ, D_MODEL)),
        "cache_fox_k": nrm(ks[3], (N_EVEN, n_pool, PAGE_SIZE, FOX_HEADS, FOX_DH)),
        "cache_fox_v": nrm(ks[4], (N_EVEN, n_pool, PAGE_SIZE, FOX_HEADS, FOX_DH)),
        "cache_fox_logf": jax.nn.log_sigmoid(
            head_bias + 0.5 * nrm(ks[5], (N_EVEN, n_pool, PAGE_SIZE, FOX_HEADS))),
        "state_gla": 0.5 * nrm(ks[6], (N_EVEN, DEC_BATCH, GLA_HEADS, GLA_DK, GLA_DV)),
        "state_pool": nrm(ks[7], (N_ODD, DEC_BATCH, POOL_BUF, D_MODEL)),
        "page_table": page_table,
        "norm_mix": 1.0 + 0.1 * nrm(ks[8], (DEPTH, D_MODEL)),
        "norm_mlp": 1.0 + 0.1 * nrm(ks[9], (DEPTH, D_MODEL)),
        "norm_final": 1.0 + 0.1 * nrm(ks[10], (D_MODEL,)),
        "w_in_even": nrm(ks[11], (N_EVEN, D_MODEL, IN_COLS)) * D_MODEL ** -0.5,
        "w_gate_up": nrm(ks[12], (N_EVEN, GLA_RANK, GLA_HEADS * GLA_DK)) * GLA_RANK ** -0.5,
        "b_gate": 0.1 * nrm(ks[13], (N_EVEN, GLA_HEADS * GLA_DK)),
        "gla_norm": 1.0 + 0.1 * nrm(ks[14], (N_EVEN, GLA_DV)),
        "b_forget": head_bias + 0.3 * nrm(ks[15], (N_EVEN, FOX_HEADS)),
        "w_out_even": nrm(ks[16], (N_EVEN, D_MODEL, D_MODEL)) * D_MODEL ** -0.5,
        "w_pool": nrm(ks[17], (N_ODD, POOL_GROUPS, POOL_GC, POOL_GC)) * POOL_GC ** -0.5,
        "pool_scale": 1.0 + 0.1 * nrm(ks[18], (N_ODD, D_MODEL)),
        "w_mlp_up": nrm(ks[19], (DEPTH, D_MODEL, D_FF)) * D_MODEL ** -0.5,
        "w_mlp_down": nrm(ks[20], (DEPTH, D_FF, D_MODEL)) * D_FF ** -0.5,
    }


def reference(x_prompt, x_sample, cache_fox_k, cache_fox_v, cache_fox_logf, state_gla, state_pool, page_table,
              norm_mix, norm_mlp, norm_final, w_in_even, w_gate_up, b_gate, gla_norm, b_forget, w_out_even,
              w_pool, pool_scale, w_mlp_up, w_mlp_down):
    xp, xs = x_prompt, x_sample
    Bp, Bs = xp.shape[0], xs.shape[0]
    past_len = page_table.shape[1] * PAGE_SIZE
    kp_l, vp_l, fp_l, ks_l, vs_l, fs_l = [], [], [], [], [], []
    gp_l, gs_l, pp_l, ps_l = [], [], [], []
    for layer in range(DEPTH):
        hp = rmsnorm(xp, norm_mix[layer])
        hs = rmsnorm(xs, norm_mix[layer])
        if layer % 2 == 0:
            e = layer // 2
            qa, ka, va, la, ga, qb, kb, vb, fb = even_project(hp, w_in_even[e], w_gate_up[e], b_gate[e], b_forget[e])
            S0 = jnp.zeros((Bp, GLA_HEADS, GLA_DK, GLA_DV), state_gla.dtype)
            oa, Sp = gla_scan(qa, ka, va, la, S0)
            ob = fox_prompt(qb, kb, vb, fb)
            xp = xp + even_merge(oa, ga, ob, gla_norm[e], w_out_even[e])
            kp_l.append(kb); vp_l.append(vb); fp_l.append(fb); gp_l.append(Sp)
            qa, ka, va, la, ga, qb, kb, vb, fb = even_project(hs, w_in_even[e], w_gate_up[e], b_gate[e], b_forget[e])
            oa, Ss = gla_scan(qa, ka, va, la, state_gla[e])
            k_past = cache_fox_k[e][page_table].reshape(Bs, past_len, FOX_HEADS, FOX_DH)
            v_past = cache_fox_v[e][page_table].reshape(Bs, past_len, FOX_HEADS, FOX_DH)
            f_past = cache_fox_logf[e][page_table].reshape(Bs, past_len, FOX_HEADS)
            ob = fox_sample(qb, kb, vb, fb, k_past, v_past, f_past)
            xs = xs + even_merge(oa, ga, ob, gla_norm[e], w_out_even[e])
            ks_l.append(kb); vs_l.append(vb); fs_l.append(fb); gs_l.append(Ss)
        else:
            o = layer // 2
            yp, bufp = pool_mix(hp, jnp.zeros((Bp, POOL_BUF, D_MODEL), hp.dtype), 0, w_pool[o], pool_scale[o])
            ys, bufs = pool_mix(hs, state_pool[o], past_len, w_pool[o], pool_scale[o])
            xp = xp + yp
            xs = xs + ys
            pp_l.append(bufp); ps_l.append(bufs)
        xp = xp + sq_relu_mlp(rmsnorm(xp, norm_mlp[layer]), w_mlp_up[layer], w_mlp_down[layer])
        xs = xs + sq_relu_mlp(rmsnorm(xs, norm_mlp[layer]), w_mlp_up[layer], w_mlp_down[layer])
    y_prompt = rmsnorm(xp, norm_final)
    y_sample = rmsnorm(xs, norm_final)
    return (y_prompt, y_sample,
            jnp.stack(kp_l), jnp.stack(vp_l), jnp.stack(fp_l),
            jnp.stack(ks_l), jnp.stack(vs_l), jnp.stack(fs_l),
            jnp.stack(gp_l), jnp.stack(gs_l),
            jnp.stack(pp_l), jnp.stack(ps_l))
```

```python
import functools

import numpy as np
import jax
import jax.numpy as jnp
from jax import lax
from jax.experimental import pallas as pl
from jax.experimental.pallas import tpu as pltpu

F32 = jnp.float32
BF16 = jnp.bfloat16

GLA_HEADS = 4
GLA_RANK = 16
GLA_TAU = 16.0
FOX_HEADS = 4
POOL_WINDOWS = (2, 4, 8, 16)
POOL_BUF = 15
PAGE_SIZE = 128
EPS = 1e-6

LANES = 128
SUBLANES = 8
VMEM_LIMIT_BYTES = 56 * 1024 * 1024

SAMPLE_PAD = SUBLANES
GLA_CHUNK = 128
GLA_SAFE_LOG_DECAY = -60.0
EX_RANK_LANE = 0
EX_FORGET_LANE = 16
AUG_GROUP = 8
ROW_TILE = 512
FOX_TILE = 512
PAGES_PER_STEP = 8


def _dot(a, b):
    return jnp.dot(a, b, preferred_element_type=F32)


def _dot_nt(a, b):
    return lax.dot_general(a, b, (((1,), (1,)), ((), ())), preferred_element_type=F32)


def _dot_tn(a, b):
    return lax.dot_general(a, b, (((0,), (0,)), ((), ())), preferred_element_type=F32)


def _split3(x):
    hi = x.astype(BF16)
    r1 = x - hi.astype(F32)
    mid = r1.astype(BF16)
    lo = (r1 - mid.astype(F32)).astype(BF16)
    return hi, mid, lo


def _dot_left01(m01, x):
    hi, mid, lo = _split3(x)
    return _dot(m01, hi) + _dot(m01, mid) + _dot(m01, lo)


def _dot_right01(x, m01):
    hi, mid, lo = _split3(x)
    return _dot(hi, m01) + _dot(mid, m01) + _dot(lo, m01)


def _log_sigmoid(x):
    return jnp.minimum(x, 0.0) - jnp.log1p(jnp.exp(-jnp.abs(x)))


def _rms_scale(x):
    return lax.rsqrt(jnp.mean(x * x, axis=-1, keepdims=True) + EPS)


def _params(*semantics):
    return pltpu.CompilerParams(dimension_semantics=semantics, vmem_limit_bytes=VMEM_LIMIT_BYTES)


def _const_spec(shape):
    nd = len(shape)
    return pl.BlockSpec(shape, lambda *_: (0,) * nd, pipeline_mode=pl.Buffered(1))


def _proj_kernel(x_ref, g_ref, w_ref, wg_ref, bg_ref, bf_ref, tseg_ref, place_ref,
                 qk_ref, va_ref, ga_ref, qb_ref, kb_ref, vb_ref, la_ref, lf_ref, fc_ref,
                 *rest, tiles_per_seq, fox_scale, with_aug):
    if with_aug:
        qaug_ref, kaug_ref, carry_ref = rest
    else:
        (carry_ref,) = rest
    i = pl.program_id(0)
    x = x_ref[...]
    h = (x * _rms_scale(x) * g_ref[...]).astype(BF16)
    d_qk = qk_ref.shape[1]

    y = _dot(h, w_ref[:, 0:d_qk])
    lane = lax.broadcasted_iota(jnp.int32, y.shape, 1)
    gla_scale = (d_qk // 2 // GLA_HEADS) ** -0.5
    qk_ref[...] = (y * jnp.where(lane < d_qk // 2, gla_scale, 1.0)).astype(qk_ref.dtype)
    c0 = d_qk
    va_ref[...] = _dot(h, w_ref[:, c0:c0 + 512]).astype(va_ref.dtype)
    ga_ref[...] = _dot(h, w_ref[:, c0 + 512:c0 + 1024]).astype(ga_ref.dtype)
    qb_ref[...] = (_dot(h, w_ref[:, c0 + 1024:c0 + 1536]) * fox_scale).astype(qb_ref.dtype)
    kb_ref[...] = _dot(h, w_ref[:, c0 + 1536:c0 + 2048])
    vb_ref[...] = _dot(h, w_ref[:, c0 + 2048:c0 + 2560])
    ex = _dot(h, w_ref[:, c0 + 2560:c0 + 2560 + LANES])

    gate = _dot(ex.astype(BF16), wg_ref[...]) + bg_ref[...]
    la_ref[...] = _log_sigmoid(gate) * (1.0 / GLA_TAU)
    lf = _log_sigmoid(ex + bf_ref[...])
    lf_ref[...] = lf

    if tiles_per_seq > 1:
        @pl.when(i % tiles_per_seq == 0)
        def _():
            carry_ref[...] = jnp.zeros_like(carry_ref)
        fc = _dot_left01(tseg_ref[...], lf) + carry_ref[0:1, :]
        carry_ref[0:1, :] = fc[fc.shape[0] - 1:, :]
    else:
        fc = _dot_left01(tseg_ref[...], lf)
    fc_ref[...] = fc

    if with_aug:
        hi, mid, lo = _split3(fc)
        placed = _dot(jnp.concatenate([hi, mid, lo], axis=1), place_ref[...])
        sub = lax.broadcasted_iota(jnp.int32, (x.shape[0], LANES), 1) % AUG_GROUP
        qaug_ref[...] = (placed[:, :LANES] + jnp.where((sub >= 3) & (sub < 6), 1.0, 0.0)).astype(BF16)
        kaug_ref[...] = (jnp.where(sub < 3, 1.0, 0.0) - placed[:, LANES:]).astype(BF16)


def _segment_tril(tile, seg):
    r = np.arange(tile)
    return jnp.asarray(((r[None, :] <= r[:, None]) & (r[None, :] // seg == r[:, None] // seg)).astype(np.float32), BF16)


def _placement_matrix():
    m = np.zeros((3 * LANES, 2 * LANES), np.float32)
    for part in range(3):
        for head in range(FOX_HEADS):
            m[part * LANES + EX_FORGET_LANE + head, AUG_GROUP * head + part] = 1.0
            m[part * LANES + EX_FORGET_LANE + head, LANES + AUG_GROUP * head + 3 + part] = 1.0
    return jnp.asarray(m, BF16)


def _project(x, g, w, wg, bg, bf, *, seq_rows, row_tile, act_dtype, with_aug):
    n, d = x.shape
    d_qk = 2 * wg.shape[1]
    tiles_per_seq = max(seq_rows // row_tile, 1)
    tseg = _segment_tril(row_tile, min(seq_rows, row_tile))
    place = _placement_matrix()
    row = lambda width: pl.BlockSpec((row_tile, width), lambda i: (i, 0))
    out_shape = [
        jax.ShapeDtypeStruct((n, d_qk), act_dtype), jax.ShapeDtypeStruct((n, 512), act_dtype),
        jax.ShapeDtypeStruct((n, 512), act_dtype), jax.ShapeDtypeStruct((n, 512), act_dtype),
        jax.ShapeDtypeStruct((n, 512), F32), jax.ShapeDtypeStruct((n, 512), F32),
        jax.ShapeDtypeStruct((n, d_qk // 2), F32), jax.ShapeDtypeStruct((n, LANES), F32),
        jax.ShapeDtypeStruct((n, LANES), F32)]
    out_specs = [row(d_qk), row(512), row(512), row(512), row(512), row(512), row(d_qk // 2),
                 row(LANES), row(LANES)]
    if with_aug:
        out_shape += [jax.ShapeDtypeStruct((n, LANES), BF16)] * 2
        out_specs += [row(LANES), row(LANES)]
    kern = functools.partial(_proj_kernel, tiles_per_seq=tiles_per_seq,
                             fox_scale=float(LANES) ** -0.5, with_aug=with_aug)
    return pl.pallas_call(
        kern, grid=(n // row_tile,),
        in_specs=[row(d), _const_spec(g.shape), _const_spec(w.shape), _const_spec(wg.shape),
                  _const_spec(bg.shape), _const_spec(bf.shape), _const_spec(tseg.shape),
                  _const_spec(place.shape)],
        out_specs=out_specs, out_shape=out_shape,
        scratch_shapes=[pltpu.VMEM((SUBLANES, LANES), F32)],
        compiler_params=_params("arbitrary"), name="even_project",
    )(x, g, w, wg, bg, bf, tseg, place)


def _gla_kernel(q_ref, k_ref, la_ref, v_ref, g_ref, s0_ref, gn_ref, tril_ref, ind_ref,
                o_ref, sn_ref, s_ref, a_ref, kk_ref, bb_ref, *, rows, real_rows):
    C = GLA_CHUNK
    dk = s0_ref.shape[1]
    dv = s0_ref.shape[2]
    n_chunks = max(rows // C, 1)

    s_ref[...] = jnp.zeros_like(s_ref)
    s_ref[0:dk, 0:dv] = s0_ref[0]
    s_ref[dk:2 * dk, dv:2 * dv] = s0_ref[1]

    lane = lax.broadcasted_iota(jnp.int32, (C, 2 * dk), 1)
    rowi = lax.broadcasted_iota(jnp.int32, (C, C), 0)
    coli = lax.broadcasted_iota(jnp.int32, (C, C), 1)
    causal = coli <= rowi
    sr = lax.broadcasted_iota(jnp.int32, (2 * dk, 2 * dv), 0)
    sc = lax.broadcasted_iota(jnp.int32, (2 * dk, 2 * dv), 1)
    same_head = (sr // dk) == (sc // dv)

    def load(ref, c, dtype):
        if rows >= C:
            return ref[pl.ds(pl.multiple_of(c * C, C), C), :].astype(dtype)
        x = ref[...].astype(dtype)
        return jnp.concatenate([x, jnp.zeros((C - rows, x.shape[1]), dtype)], axis=0)

    def chunk(c, carry):
        q = load(q_ref, c, F32)
        k = load(k_ref, c, F32)
        la = load(la_ref, c, F32)
        if real_rows < rows or rows < C:
            real = lax.broadcasted_iota(jnp.int32, la.shape, 0) < real_rows
            la = jnp.where(real, la, 0.0)
            k = jnp.where(real, k, 0.0)
        v = load(v_ref, c, BF16)
        b = _dot_left01(tril_ref[...], la)
        b_last = b[C - 1:C, :]
        q_in = q * jnp.exp(b)
        k_hat = (k * jnp.exp(b_last - b)).astype(BF16)
        safe = jnp.min(b_last) >= GLA_SAFE_LOG_DECAY

        @pl.when(safe)
        def _():
            k_out = (k * jnp.exp(-b)).astype(BF16)
            for hh in range(2):
                qm = jnp.where((lane // dk) == hh, q_in, 0.0).astype(BF16)
                a_ref[hh * C:(hh + 1) * C, :] = _dot_nt(qm, k_out)

        @pl.when(jnp.logical_not(safe))
        def _():
            kk_ref[...] = k
            bb_ref[...] = b
            a_ref[...] = jnp.zeros_like(a_ref)

            def key_row(s, carry2):
                ks = kk_ref[pl.ds(s, 1), :]
                bs = bb_ref[pl.ds(s, 1), :]
                e = q * jnp.exp(jnp.minimum(b - bs, 0.0)) * ks
                sums = _dot(e.astype(BF16), ind_ref[...])
                hit = coli == s
                a_ref[0:C, :] += jnp.where(hit, sums[:, 0:1], 0.0)
                a_ref[C:2 * C, :] += jnp.where(hit, sums[:, 1:2], 0.0)
                return carry2

            lax.fori_loop(0, C, key_row, 0)

        s_bd = s_ref[...]
        o_inter = _dot(q_in.astype(BF16), s_bd.astype(BF16))
        gn = gn_ref[...]
        g = load(g_ref, c, F32)
        outs = []
        for hh in range(2):
            a = jnp.where(causal, a_ref[hh * C:(hh + 1) * C, :], 0.0).astype(BF16)
            o = _dot(a, v[:, hh * dv:(hh + 1) * dv]) + o_inter[:, hh * dv:(hh + 1) * dv]
            gh = g[:, hh * dv:(hh + 1) * dv]
            outs.append(o * _rms_scale(o) * gn * (gh * jax.nn.sigmoid(gh)))
        o_full = jnp.concatenate(outs, axis=1)
        if rows >= C:
            o_ref[pl.ds(pl.multiple_of(c * C, C), C), :] = o_full.astype(o_ref.dtype)
        else:
            o_ref[...] = o_full[0:rows, :].astype(o_ref.dtype)

        upd = _dot_tn(k_hat, v)
        decay = jnp.exp(jnp.broadcast_to(b_last, (2 * dk, 2 * dk)).T)
        decay2 = jnp.concatenate([decay] * (2 * dv // (2 * dk)), axis=1)
        s_ref[...] = jnp.where(same_head, s_bd * decay2 + upd, 0.0)
        return carry

    lax.fori_loop(0, n_chunks, chunk, 0)
    sn_ref[0] = s_ref[0:dk, 0:dv]
    sn_ref[1] = s_ref[dk:2 * dk, dv:2 * dv]


def _gla(qk, la, va, ga, s0, gn, *, n_seq, rows, real_rows, out_dtype):
    n = qk.shape[0]
    dk, dv = s0.shape[2], s0.shape[3]
    pairs = GLA_HEADS // 2
    C = GLA_CHUNK
    tril = _segment_tril(C, C)
    ind = np.zeros((2 * dk, LANES), np.float32)
    ind[:dk, 0] = 1.0
    ind[dk:, 1] = 1.0
    ind = jnp.asarray(ind, BF16)
    kern = functools.partial(_gla_kernel, rows=rows, real_rows=real_rows)
    o, sn = pl.pallas_call(
        kern, grid=(n_seq, pairs),
        in_specs=[pl.BlockSpec((rows, 2 * dk), lambda b, p: (b, p)),
                  pl.BlockSpec((rows, 2 * dk), lambda b, p: (b, pairs + p)),
                  pl.BlockSpec((rows, 2 * dk), lambda b, p: (b, p)),
                  pl.BlockSpec((rows, 2 * dv), lambda b, p: (b, p)),
                  pl.BlockSpec((rows, 2 * dv), lambda b, p: (b, p)),
                  pl.BlockSpec((None, 2, dk, dv), lambda b, p: (b, p, 0, 0)),
                  _const_spec(gn.shape), _const_spec(tril.shape), _const_spec(ind.shape)],
        out_specs=[pl.BlockSpec((rows, 2 * dv), lambda b, p: (b, p)),
                   pl.BlockSpec((None, 2, dk, dv), lambda b, p: (b, p, 0, 0))],
        out_shape=[jax.ShapeDtypeStruct((n, GLA_HEADS * dv), out_dtype),
                   jax.ShapeDtypeStruct(s0.shape, F32)],
        scratch_shapes=[pltpu.VMEM((2 * dk, 2 * dv), F32), pltpu.VMEM((2 * C, C), F32),
                        pltpu.VMEM((C, 2 * dk), F32), pltpu.VMEM((C, 2 * dk), F32)],
        compiler_params=_params("arbitrary", "arbitrary"), name="gla_scan",
    )(qk, qk, la, va, ga, s0, gn, tril, ind)
    return o, sn


def _fox_prompt_kernel(q_ref, qa_ref, k_ref, v_ref, ka_ref, o_ref, m_ref, l_ref, acc_ref):
    head = pl.program_id(1)
    qi = pl.program_id(2)
    tq = q_ref.shape[0]
    lane = lax.broadcasted_iota(jnp.int32, qa_ref.shape, 1)
    qa = jnp.where((lane // AUG_GROUP) == head, qa_ref[...], jnp.zeros_like(qa_ref[...]))
    qf = jnp.concatenate([q_ref[...], qa], axis=1)
    m_ref[...] = jnp.full_like(m_ref, -jnp.inf)
    l_ref[...] = jnp.zeros_like(l_ref)
    acc_ref[...] = jnp.zeros_like(acc_ref)

    def update(j, masked):
        rows = pl.ds(pl.multiple_of(j * tq, tq), tq)
        kf = jnp.concatenate([k_ref[rows, :].astype(BF16), ka_ref[rows, :]], axis=1)
        s = _dot_nt(qf, kf)
        if masked:
            r = lax.broadcasted_iota(jnp.int32, s.shape, 0)
            c = lax.broadcasted_iota(jnp.int32, s.shape, 1)
            s = jnp.where(c <= r, s, -jnp.inf)
        m_old = m_ref[...]
        m_new = jnp.maximum(m_old, jnp.max(s, axis=1, keepdims=True))
        p = jnp.exp(s - m_new)
        alpha = jnp.exp(m_old - m_new)
        l_ref[...] = alpha * l_ref[...] + jnp.sum(p, axis=1, keepdims=True)
        acc_ref[...] = alpha * acc_ref[...] + _dot(p.astype(BF16), v_ref[rows, :].astype(BF16))
        m_ref[...] = m_new

    def full_tile(j, carry):
        update(j, False)
        return carry

    lax.fori_loop(0, qi, full_tile, 0)
    update(qi, True)
    o_ref[...] = (acc_ref[...] / l_ref[...]).astype(o_ref.dtype)


def _fox_prompt(qb, qaug, kb, vb, kaug, *, n_seq, seq, out_dtype):
    n = qb.shape[0]
    tq = FOX_TILE
    nq = seq // tq
    return pl.pallas_call(
        _fox_prompt_kernel, grid=(n_seq, FOX_HEADS, nq),
        in_specs=[pl.BlockSpec((tq, LANES), lambda b, h, i: (b * nq + i, h)),
                  pl.BlockSpec((tq, LANES), lambda b, h, i: (b * nq + i, 0)),
                  pl.BlockSpec((seq, LANES), lambda b, h, i: (b, h)),
                  pl.BlockSpec((seq, LANES), lambda b, h, i: (b, h)),
                  pl.BlockSpec((seq, LANES), lambda b, h, i: (b, 0))],
        out_specs=pl.BlockSpec((tq, LANES), lambda b, h, i: (b * nq + i, h)),
        out_shape=jax.ShapeDtypeStruct((n, FOX_HEADS * LANES), out_dtype),
        scratch_shapes=[pltpu.VMEM((tq, 1), F32), pltpu.VMEM((tq, 1), F32), pltpu.VMEM((tq, LANES), F32)],
        compiler_params=_params("arbitrary", "arbitrary", "arbitrary"), name="fox_prompt",
    )(qb, qaug, kb, vb, kaug)


def _fox_bias_kernel(pt_ref, cache_ref, fc_ref, m12_ref, upper_ref, pm_ref, ones_ref,
                     rh_ref, cmask_ref, bnew_ref, g_ref, *, real_rows):
    b = pl.program_id(0)
    n_pages = pt_ref.shape[1]
    width = cache_ref.shape[1]

    def gather(p, carry):
        g_ref[pl.ds(p, 1), :] = cache_ref[pl.ds(pt_ref[b, p], 1), :]
        return carry

    lax.fori_loop(0, n_pages, gather, 0)
    w = _dot_right01(g_ref[...], m12_ref[...])
    rh_ref[...] = w[:, :width] + _dot_left01(upper_ref[...], w[:, width:])

    f8 = fc_ref[...]
    rows32 = FOX_HEADS * SAMPLE_PAD
    r = lax.broadcasted_iota(jnp.int32, (rows32, LANES), 0)
    c = lax.broadcasted_iota(jnp.int32, (rows32, LANES), 1)
    own = c == (EX_FORGET_LANE + r // SAMPLE_PAD)
    f_query = _dot_right01(jnp.where(own, jnp.concatenate([f8] * FOX_HEADS, axis=0), 0.0), ones_ref[...])
    r8 = lax.broadcasted_iota(jnp.int32, (SAMPLE_PAD, LANES), 0)
    c8 = lax.broadcasted_iota(jnp.int32, (SAMPLE_PAD, LANES), 1)
    f_key = jnp.sum(jnp.where(r8 == c8 // FOX_HEADS, _dot_right01(f8, pm_ref[...]), 0.0), axis=0, keepdims=True)
    key_head, key_t = c % FOX_HEADS, c // FOX_HEADS
    head, t = r // SAMPLE_PAD, r % SAMPLE_PAD
    valid = (key_head == head) & (key_t <= t) & (key_t < real_rows)
    bnew_ref[...] = jnp.where(valid, f_query - f_key, -jnp.inf)
    cw = lax.broadcasted_iota(jnp.int32, (rows32, width), 1)
    rw = lax.broadcasted_iota(jnp.int32, (rows32, width), 0)
    cmask_ref[...] = jnp.where(cw % FOX_HEADS == rw // SAMPLE_PAD,
                               jnp.concatenate([f_query] * (width // LANES), axis=1), -jnp.inf)


def _fox_bias(page_table, cache_logf, fc_s, *, real_rows):
    n_seq, n_pages = page_table.shape
    width = cache_logf.shape[1]
    i_head, i_pos = np.arange(width) // PAGE_SIZE, np.arange(width) % PAGE_SIZE
    j_pos, j_head = np.arange(width) // FOX_HEADS, np.arange(width) % FOX_HEADS
    same = i_head[:, None] == j_head[None, :]
    m12 = jnp.asarray(np.concatenate([same & (i_pos[:, None] > j_pos[None, :]), same], axis=1).astype(np.float32), BF16)
    pg = np.arange(n_pages)
    upper = jnp.asarray((pg[None, :] > pg[:, None]).astype(np.float32), BF16)
    ln = np.arange(LANES)
    pm = jnp.asarray((ln[:, None] == EX_FORGET_LANE + ln[None, :] % FOX_HEADS).astype(np.float32), BF16)
    ones = jnp.ones((LANES, LANES), BF16)
    rows32 = FOX_HEADS * SAMPLE_PAD
    const = lambda a: pl.BlockSpec(a.shape, lambda b, pt: (0,) * a.ndim, pipeline_mode=pl.Buffered(1))
    grid_spec = pltpu.PrefetchScalarGridSpec(
        num_scalar_prefetch=1, grid=(n_seq,),
        in_specs=[const(cache_logf), pl.BlockSpec((SAMPLE_PAD, LANES), lambda b, pt: (b, 0)),
                  const(m12), const(upper), const(pm), const(ones)],
        out_specs=[pl.BlockSpec((None, n_pages, width), lambda b, pt: (b, 0, 0)),
                   pl.BlockSpec((None, rows32, width), lambda b, pt: (b, 0, 0)),
                   pl.BlockSpec((None, rows32, LANES), lambda b, pt: (b, 0, 0))],
        scratch_shapes=[pltpu.VMEM((n_pages, width), F32)])
    return pl.pallas_call(
        functools.partial(_fox_bias_kernel, real_rows=real_rows), grid_spec=grid_spec,
        out_shape=[jax.ShapeDtypeStruct((n_seq, n_pages, width), F32),
                   jax.ShapeDtypeStruct((n_seq, rows32, width), F32),
                   jax.ShapeDtypeStruct((n_seq, rows32, LANES), F32)],
        compiler_params=_params("arbitrary"), name="fox_sample_bias",
    )(page_table, cache_logf, fc_s, m12, upper, pm, ones)


def _fox_sample_kernel(pt_ref, q_ref, rh_ref, cmask_ref, bnew_ref, kn_ref, vn_ref, *rest, n_group):
    k_refs = rest[:n_group]
    v_refs = rest[n_group:2 * n_group]
    o_ref, m_ref, l_ref, acc_ref = rest[2 * n_group:]
    g = pl.program_id(1)
    q8 = q_ref[...]
    q = jnp.concatenate([q8[:, hh * LANES:(hh + 1) * LANES] for hh in range(FOX_HEADS)], axis=0).astype(BF16)
    cmask = cmask_ref[...]

    @pl.when(g == 0)
    def _():
        m_ref[...] = jnp.full_like(m_ref, -jnp.inf)
        l_ref[...] = jnp.zeros_like(l_ref)
        acc_ref[...] = jnp.zeros_like(acc_ref)

    def update(scores, values):
        s = jnp.concatenate(scores, axis=1) if len(scores) > 1 else scores[0]
        m_old = m_ref[...]
        m_new = jnp.maximum(m_old, jnp.max(s, axis=1, keepdims=True))
        p32 = jnp.exp(s - m_new)
        p = p32.astype(BF16)
        alpha = jnp.exp(m_old - m_new)
        l_ref[...] = alpha * l_ref[...] + jnp.sum(p32, axis=1, keepdims=True)
        pv, off = None, 0
        for v in values:
            part = _dot(p[:, off:off + v.shape[0]], v)
            pv = part if pv is None else pv + part
            off += v.shape[0]
        acc_ref[...] = alpha * acc_ref[...] + pv
        m_ref[...] = m_new

    scores, values = [], []
    for j in range(n_group):
        scores.append(_dot_nt(q, k_refs[j][...].astype(BF16)) + rh_ref[j:j + 1, :] + cmask)
        values.append(v_refs[j][...].astype(BF16))
    update(scores, values)

    @pl.when(g == pl.num_programs(1) - 1)
    def _():
        pad = jnp.zeros((LANES - kn_ref.shape[0], kn_ref.shape[1]), BF16)
        kn = jnp.concatenate([kn_ref[...].astype(BF16), pad], axis=0)
        vn = jnp.concatenate([vn_ref[...].astype(BF16), pad], axis=0)
        update([_dot_nt(q, kn) + bnew_ref[...]], [vn])
        o = acc_ref[...] / l_ref[...]
        o_ref[...] = jnp.concatenate(
            [o[hh * SAMPLE_PAD:(hh + 1) * SAMPLE_PAD, :] for hh in range(FOX_HEADS)], axis=1).astype(o_ref.dtype)


def _fox_sample(page_table, qb_s, rh, cmask, bnew, kn, vn, cache_k, cache_v, *, layer):
    n_seq, n_pages = page_table.shape
    G = PAGES_PER_STEP
    rows32 = cmask.shape[1]
    width = qb_s.shape[1]
    page_rows, dh = cache_k.shape[2], cache_k.shape[3]

    def page_spec(j):
        return pl.BlockSpec((None, None, page_rows, dh), lambda b, g, pt: (layer, pt[b, g * G + j], 0, 0))

    grid_spec = pltpu.PrefetchScalarGridSpec(
        num_scalar_prefetch=1, grid=(n_seq, n_pages // G),
        in_specs=[pl.BlockSpec((SAMPLE_PAD, width), lambda b, g, pt: (b, 0)),
                  pl.BlockSpec((None, G, page_rows), lambda b, g, pt: (b, g, 0)),
                  pl.BlockSpec((None, rows32, page_rows), lambda b, g, pt: (b, 0, 0)),
                  pl.BlockSpec((None, rows32, LANES), lambda b, g, pt: (b, 0, 0)),
                  pl.BlockSpec((rows32, dh), lambda b, g, pt: (b, 0)),
                  pl.BlockSpec((rows32, dh), lambda b, g, pt: (b, 0))]
                 + [page_spec(j) for j in range(G)] + [page_spec(j) for j in range(G)],
        out_specs=pl.BlockSpec((SAMPLE_PAD, width), lambda b, g, pt: (b, 0)),
        scratch_shapes=[pltpu.VMEM((rows32, 1), F32), pltpu.VMEM((rows32, 1), F32),
                        pltpu.VMEM((rows32, dh), F32)])
    return pl.pallas_call(
        functools.partial(_fox_sample_kernel, n_group=G), grid_spec=grid_spec,
        out_shape=jax.ShapeDtypeStruct((n_seq * SAMPLE_PAD, width), F32),
        compiler_params=_params("arbitrary", "arbitrary"), name="fox_sample",
    )(page_table, qb_s, rh, cmask, bnew, kn, vn, *([cache_k] * G), *([cache_v] * G))


def _pool_kernel(x_ref, g_ref, buf_ref, w_ref, sc_ref, xo_ref, tail_ref, halo_ref, *, pos0):
    i = pl.program_id(1)
    tm = x_ref.shape[0]
    halo = halo_ref.shape[0]
    gc = w_ref.shape[1]

    @pl.when(i == 0)
    def _():
        halo_ref[...] = buf_ref[...]

    x = x_ref[...]
    h = x * _rms_scale(x) * g_ref[...]
    ext = jnp.concatenate([halo_ref[...], h], axis=0)
    tail_ref[...] = ext[tm + halo - tail_ref.shape[0]:, :]
    if tm >= halo:
        halo_ref[...] = h[tm - halo:, :]
    pos = pos0 + i * tm + lax.broadcasted_iota(jnp.int32, (tm, gc), 0) + 1
    ys = []
    for grp, win in enumerate(POOL_WINDOWS):
        acc = ext[:, grp * gc:(grp + 1) * gc]
        shift = 1
        while shift < win:
            acc = acc + pltpu.roll(acc, shift, axis=0)
            shift *= 2
        cnt = jnp.minimum(win, pos).astype(F32)
        pooled = acc[halo:, :] / cnt - h[:, grp * gc:(grp + 1) * gc]
        ys.append(_dot(pooled.astype(BF16), w_ref[grp]))
    xo_ref[...] = x + jnp.concatenate(ys, axis=1) * sc_ref[...]


def _pool(x, g, buf16, w, sc, *, n_seq, rows, row_tile, pos0):
    n, d = x.shape
    nt = rows // row_tile
    tail_rows = 3 * SUBLANES
    return pl.pallas_call(
        functools.partial(_pool_kernel, pos0=pos0), grid=(n_seq, nt),
        in_specs=[pl.BlockSpec((row_tile, d), lambda b, i: (b * nt + i, 0)),
                  _const_spec(g.shape),
                  pl.BlockSpec((None,) + buf16.shape[1:], lambda b, i: (b, 0, 0)),
                  _const_spec(w.shape), _const_spec(sc.shape)],
        out_specs=[pl.BlockSpec((row_tile, d), lambda b, i: (b * nt + i, 0)),
                   pl.BlockSpec((None, tail_rows, d), lambda b, i: (b, 0, 0))],
        out_shape=[jax.ShapeDtypeStruct((n, d), F32), jax.ShapeDtypeStruct((n_seq, tail_rows, d), F32)],
        scratch_shapes=[pltpu.VMEM(buf16.shape[1:], F32)],
        compiler_params=_params("arbitrary", "arbitrary"), name="pool_mix",
    )(x, g, buf16, w, sc)


def _mlp_kernel(*refs, with_merge, with_final):
    if with_merge:
        x_ref, oa_ref, ob_ref, wo_ref, g_ref, wu_ref, wd_ref, gf_ref, o_ref, acc_ref = refs
        half = oa_ref.shape[1]
        x = (x_ref[...] + _dot(oa_ref[...].astype(BF16), wo_ref[0:half, :])
             + _dot(ob_ref[...].astype(BF16), wo_ref[half:, :]))
    else:
        x_ref, g_ref, wu_ref, wd_ref, gf_ref, o_ref, acc_ref = refs
        x = x_ref[...]
    h = (x * _rms_scale(x) * g_ref[...]).astype(BF16)
    acc_ref[...] = x

    def ff_chunk(c, carry):
        a = jnp.maximum(_dot(h, wu_ref[c]), 0.0)
        acc_ref[...] += _dot((a * a).astype(BF16), wd_ref[c])
        return carry

    lax.fori_loop(0, wu_ref.shape[0], ff_chunk, 0)
    y = acc_ref[...]
    if with_final:
        y = y * _rms_scale(y) * gf_ref[...]
    o_ref[...] = y


def _mlp(x, g, wu, wd, gf, *, row_tile, merge=None, with_final=False):
    n, d = x.shape
    row = lambda width: pl.BlockSpec((row_tile, width), lambda i: (i, 0))
    args, specs = [x], [row(d)]
    if merge is not None:
        oa, ob, wo = merge
        args += [oa, ob, wo]
        specs += [row(oa.shape[1]), row(ob.shape[1]), _const_spec(wo.shape)]
    args += [g, wu, wd, gf]
    specs += [_const_spec(g.shape), _const_spec(wu.shape), _const_spec(wd.shape), _const_spec(gf.shape)]
    return pl.pallas_call(
        functools.partial(_mlp_kernel, with_merge=merge is not None, with_final=with_final),
        grid=(n // row_tile,), in_specs=specs, out_specs=row(d),
        out_shape=jax.ShapeDtypeStruct((n, d), F32),
        scratch_shapes=[pltpu.VMEM((row_tile, d), F32)],
        compiler_params=_params("arbitrary"), name="mlp",
    )(*args)


def _reorder_in_proj(w_in, dk_total, dv_total, fox_total):
    o = 0
    q_a = w_in[:, o:o + dk_total]; o += dk_total
    k_a = w_in[:, o:o + dk_total]; o += dk_total
    v_a = w_in[:, o:o + dv_total]; o += dv_total
    r_a = w_in[:, o:o + GLA_RANK]; o += GLA_RANK
    g_a = w_in[:, o:o + dv_total]; o += dv_total
    q_b = w_in[:, o:o + fox_total]; o += fox_total
    k_b = w_in[:, o:o + fox_total]; o += fox_total
    v_b = w_in[:, o:o + fox_total]; o += fox_total
    f_b = w_in[:, o:o + FOX_HEADS]
    pad = jnp.zeros((w_in.shape[0], LANES - GLA_RANK - FOX_HEADS), w_in.dtype)
    return jnp.concatenate([q_a, k_a, v_a, g_a, q_b, k_b, v_b, r_a, f_b, pad], axis=1).astype(BF16)


def kernel(x_prompt, x_sample, cache_fox_k, cache_fox_v, cache_fox_logf, state_gla, state_pool, page_table,
           norm_mix, norm_mlp, norm_final, w_in_even, w_gate_up, b_gate, gla_norm, b_forget, w_out_even,
           w_pool, pool_scale, w_mlp_up, w_mlp_down):
    n_prompt, seq, d = x_prompt.shape
    n_sample, t_new, _ = x_sample.shape
    depth = norm_mix.shape[0]
    n_even, n_pool = cache_fox_k.shape[0], cache_fox_k.shape[1]
    fox_dh = cache_fox_k.shape[-1]
    fox_total = FOX_HEADS * fox_dh
    dk, dv = state_gla.shape[3], state_gla.shape[4]
    dk_total, dv_total = GLA_HEADS * dk, GLA_HEADS * dv
    d_ff = w_mlp_up.shape[2]
    past_len = page_table.shape[1] * PAGE_SIZE
    T8 = SAMPLE_PAD
    ff_chunk = 512

    xp = x_prompt.reshape(n_prompt * seq, d)
    xs = jnp.pad(x_sample, ((0, 0), (0, T8 - t_new), (0, 0))).reshape(n_sample * T8, d)

    row2 = lambda v: v.reshape(1, -1).astype(F32)
    wu = w_mlp_up.astype(BF16).reshape(depth, d, d_ff // ff_chunk, ff_chunk).transpose(0, 2, 1, 3)
    wd = w_mlp_down.astype(BF16).reshape(depth, d_ff // ff_chunk, ff_chunk, d)
    gfin = row2(norm_final)
    cache_k = cache_fox_k.reshape(n_even, n_pool, PAGE_SIZE * FOX_HEADS, fox_dh)
    cache_v = cache_fox_v.reshape(n_even, n_pool, PAGE_SIZE * FOX_HEADS, fox_dh)
    cache_f = cache_fox_logf.transpose(0, 1, 3, 2).reshape(n_even, n_pool, FOX_HEADS * PAGE_SIZE)

    kp, vp, fp, ks, vs, fs, gp, gs, pp, ps = ([] for _ in range(10))
    for layer in range(depth):
        g_mix = row2(norm_mix[layer])
        g_mlp = row2(norm_mlp[layer])
        last = layer == depth - 1
        if layer % 2 == 0:
            e = layer // 2
            w = _reorder_in_proj(w_in_even[e], dk_total, dv_total, fox_total)
            wg = jnp.zeros((LANES, dk_total), F32).at[EX_RANK_LANE:EX_RANK_LANE + GLA_RANK].set(
                w_gate_up[e]).astype(BF16)
            bg = row2(b_gate[e])
            bf = jnp.zeros((1, LANES), F32).at[0, EX_FORGET_LANE:EX_FORGET_LANE + FOX_HEADS].set(b_forget[e])
            gn = row2(gla_norm[e])
            wo = w_out_even[e].astype(BF16)

            qk, va, ga, qb, kb, vb, la, lf, fc, qaug, kaug = _project(
                xp, g_mix, w, wg, bg, bf, seq_rows=seq, row_tile=ROW_TILE, act_dtype=BF16, with_aug=True)
            s0 = jnp.zeros((n_prompt, GLA_HEADS, dk, dv), F32)
            oa, sp_new = _gla(qk, la, va, ga, s0, gn, n_seq=n_prompt, rows=seq, real_rows=seq, out_dtype=BF16)
            ob = _fox_prompt(qb, qaug, kb, vb, kaug, n_seq=n_prompt, seq=seq, out_dtype=BF16)
            xp = _mlp(xp, g_mlp, wu[layer], wd[layer], gfin, row_tile=ROW_TILE, merge=(oa, ob, wo),
                      with_final=last)
            kp.append(kb.reshape(n_prompt, seq, FOX_HEADS, fox_dh))
            vp.append(vb.reshape(n_prompt, seq, FOX_HEADS, fox_dh))
            fp.append(lf[:, EX_FORGET_LANE:EX_FORGET_LANE + FOX_HEADS].reshape(n_prompt, seq, FOX_HEADS))
            gp.append(sp_new.astype(state_gla.dtype))

            qk, va, ga, qb, kb, vb, la, lf, fc = _project(
                xs, g_mix, w, wg, bg, bf, seq_rows=T8, row_tile=n_sample * T8, act_dtype=F32, with_aug=False)
            oa, ss_new = _gla(qk, la, va, ga, state_gla[e].astype(F32), gn, n_seq=n_sample, rows=T8,
                              real_rows=t_new, out_dtype=F32)
            rh, cmask, bnew = _fox_bias(page_table, cache_f[e], fc, real_rows=t_new)
            ob = _fox_sample(page_table, qb, rh, cmask, bnew, kb.reshape(-1, fox_dh), vb.reshape(-1, fox_dh),
                             cache_k, cache_v, layer=e)
            xs = _mlp(xs, g_mlp, wu[layer], wd[layer], gfin, row_tile=n_sample * T8, merge=(oa, ob, wo),
                      with_final=last)
            ks.append(kb.reshape(n_sample, T8, FOX_HEADS, fox_dh)[:, :t_new])
            vs.append(vb.reshape(n_sample, T8, FOX_HEADS, fox_dh)[:, :t_new])
            fs.append(lf[:, EX_FORGET_LANE:EX_FORGET_LANE + FOX_HEADS].reshape(n_sample, T8, FOX_HEADS)[:, :t_new])
            gs.append(ss_new.astype(state_gla.dtype))
        else:
            o = layer // 2
            wpl = w_pool[o].astype(BF16)
            sc = row2(pool_scale[o])
            halo = POOL_BUF + 1
            xp, tail = _pool(xp, g_mix, jnp.zeros((n_prompt, halo, d), F32), wpl, sc, n_seq=n_prompt,
                             rows=seq, row_tile=ROW_TILE, pos0=0)
            pp.append(tail[:, tail.shape[1] - POOL_BUF:])
            buf16 = jnp.pad(state_pool[o].astype(F32), ((0, 0), (1, 0), (0, 0)))
            xs, tail = _pool(xs, g_mix, buf16, wpl, sc, n_seq=n_sample, rows=T8, row_tile=T8, pos0=past_len)
            end = tail.shape[1] - (T8 - t_new)
            ps.append(tail[:, end - POOL_BUF:end])
            xp = _mlp(xp, g_mlp, wu[layer], wd[layer], gfin, row_tile=ROW_TILE, with_final=last)
            xs = _mlp(xs, g_mlp, wu[layer], wd[layer], gfin, row_tile=n_sample * T8, with_final=last)

    y_prompt = xp.reshape(n_prompt, seq, d)
    y_sample = xs.reshape(n_sample, T8, d)[:, :t_new]
    return (y_prompt, y_sample, jnp.stack(kp), jnp.stack(vp), jnp.stack(fp),
            jnp.stack(ks), jnp.stack(vs), jnp.stack(fs), jnp.stack(gp), jnp.stack(gs),
            jnp.stack(pp), jnp.stack(ps))
```

```python
import functools

import numpy as np
import jax
import jax.numpy as jnp
from jax import lax
from jax.experimental import pallas as pl
from jax.experimental.pallas import tpu as pltpu

F32 = jnp.float32
BF16 = jnp.bfloat16

GLA_HEADS = 4
GLA_RANK = 16
GLA_TAU = 16.0
FOX_HEADS = 4
POOL_WINDOWS = (2, 4, 8, 16)
POOL_BUF = 15
PAGE_SIZE = 128
EPS = 1e-6

LANES = 128
SUBLANES = 8
VMEM_LIMIT_BYTES = 56 * 1024 * 1024

SAMPLE_PAD = SUBLANES
GLA_CHUNK = 128
GLA_SAFE_LOG_DECAY = -60.0
EX_RANK_LANE = 0
EX_FORGET_LANE = LANES - FOX_HEADS
LOG2E = 1.4426950408889634
FF_CHUNK = 512
AUG_GROUP = 8
ROW_TILE = 512
FOX_TILE = 512
PAGES_PER_STEP = 8


def _dot(a, b):
    return jnp.dot(a, b, preferred_element_type=F32)


def _dot_nt(a, b):
    return lax.dot_general(a, b, (((1,), (1,)), ((), ())), preferred_element_type=F32)


def _dot_tn(a, b):
    return lax.dot_general(a, b, (((0,), (0,)), ((), ())), preferred_element_type=F32)


def _split3(x):
    hi = x.astype(BF16)
    r1 = x - hi.astype(F32)
    mid = r1.astype(BF16)
    lo = (r1 - mid.astype(F32)).astype(BF16)
    return hi, mid, lo


def _dot_left01(m01, x):
    hi, mid, lo = _split3(x)
    return _dot(m01, hi) + _dot(m01, mid) + _dot(m01, lo)


def _dot_right01(x, m01):
    hi, mid, lo = _split3(x)
    return _dot(hi, m01) + _dot(mid, m01) + _dot(lo, m01)


def _log_sigmoid(x):
    return jnp.minimum(x, 0.0) - jnp.log1p(jnp.exp(-jnp.abs(x)))


def _rms_scale(x):
    return lax.rsqrt(jnp.mean(x * x, axis=-1, keepdims=True) + EPS)


def _params(*semantics):
    return pltpu.CompilerParams(dimension_semantics=semantics, vmem_limit_bytes=VMEM_LIMIT_BYTES)


def _const_spec(shape):
    nd = len(shape)
    return pl.BlockSpec(shape, lambda *_: (0,) * nd, pipeline_mode=pl.Buffered(1))


def _prep_in_proj_kernel(w_ref, o_ref, *, rank_off, rank, n_forget):
    x = w_ref[...]
    total = x.shape[1]
    main = jnp.concatenate([x[:, 0:rank_off], x[:, rank_off + rank:total - n_forget]], axis=1)
    lane = lax.broadcasted_iota(jnp.int32, (x.shape[0], LANES), 1)
    extras = jnp.where(lane < rank, x[:, rank_off:rank_off + LANES],
                       jnp.where(lane >= LANES - n_forget, x[:, total - LANES:total], 0.0))
    o_ref[:, 0:main.shape[1]] = main.astype(o_ref.dtype)
    o_ref[:, main.shape[1]:] = extras.astype(o_ref.dtype)


def _prep_in_proj(w_in, rank_off):
    n_even, d, total = w_in.shape
    out_cols = total - GLA_RANK - FOX_HEADS + LANES
    tile = 256
    return pl.pallas_call(
        functools.partial(_prep_in_proj_kernel, rank_off=rank_off, rank=GLA_RANK, n_forget=FOX_HEADS),
        grid=(n_even, d // tile),
        in_specs=[pl.BlockSpec((None, tile, total), lambda e, i: (e, i, 0))],
        out_specs=pl.BlockSpec((None, tile, out_cols), lambda e, i: (e, i, 0)),
        out_shape=jax.ShapeDtypeStruct((n_even, d, out_cols), BF16),
        compiler_params=_params("arbitrary", "arbitrary"), name="prep_in_proj",
    )(w_in)


def _proj_kernel(x_ref, g_ref, w_ref, wg_ref, bg_ref, bf_ref, tseg_ref, place_ref, *rest,
                 tiles_per_seq, fox_scale, with_aug, has_prev):
    if has_prev:
        rest = rest[2:]
    qk_ref, va_ref, ga_ref, qb_ref, kb_ref, vb_ref, la_ref, lf_ref, fc_ref = rest[:9]
    if with_aug:
        qaug_ref, kaug_ref, carry_ref = rest[9:]
    else:
        (carry_ref,) = rest[9:]
    i = pl.program_id(0)
    x = x_ref[...]
    h = (x * _rms_scale(x) * g_ref[...]).astype(BF16)
    d_qk = qk_ref.shape[1]

    y = _dot(h, w_ref[:, 0:d_qk])
    lane = lax.broadcasted_iota(jnp.int32, y.shape, 1)
    gla_scale = (d_qk // 2 // GLA_HEADS) ** -0.5
    qk_ref[...] = (y * jnp.where(lane < d_qk // 2, gla_scale, 1.0)).astype(qk_ref.dtype)
    c0 = d_qk
    va_ref[...] = _dot(h, w_ref[:, c0:c0 + 512]).astype(va_ref.dtype)
    ga_ref[...] = _dot(h, w_ref[:, c0 + 512:c0 + 1024]).astype(ga_ref.dtype)
    qb_ref[...] = (_dot(h, w_ref[:, c0 + 1024:c0 + 1536]) * fox_scale).astype(qb_ref.dtype)
    tm = x.shape[0]
    for out_ref, col in ((kb_ref, c0 + 1536), (vb_ref, c0 + 2048)):
        y = _dot(h, w_ref[:, col:col + 512])
        for hh in range(FOX_HEADS):
            out_ref[pl.ds(hh, tm, stride=FOX_HEADS), :] = y[:, hh * LANES:(hh + 1) * LANES]
    ex = _dot(h, w_ref[:, c0 + 2560:c0 + 2560 + LANES])

    gate = _dot(ex.astype(BF16), wg_ref[...]) + bg_ref[...]
    la_ref[...] = _log_sigmoid(gate) * (1.0 / GLA_TAU)
    lf = _log_sigmoid(ex + bf_ref[...])
    lf_ref[...] = lf

    if tiles_per_seq > 1:
        @pl.when(i % tiles_per_seq == 0)
        def _():
            carry_ref[...] = jnp.zeros_like(carry_ref)
        fc = _dot_left01(tseg_ref[...], lf) + carry_ref[0:1, :]
        carry_ref[0:1, :] = fc[fc.shape[0] - 1:, :]
    else:
        fc = _dot_left01(tseg_ref[...], lf)
    fc_ref[...] = fc

    if with_aug:
        hi, mid, lo = _split3(fc * LOG2E)
        placed = _dot(jnp.concatenate([hi, mid, lo], axis=1), place_ref[...])
        sub = lax.broadcasted_iota(jnp.int32, (x.shape[0], LANES), 1) % AUG_GROUP
        qaug_ref[...] = (placed[:, :LANES] + jnp.where((sub >= 3) & (sub < 6), 1.0, 0.0)).astype(BF16)
        kaug_ref[...] = (jnp.where(sub < 3, 1.0, 0.0) - placed[:, LANES:]).astype(BF16)


def _segment_tril(tile, seg):
    r = np.arange(tile)
    return jnp.asarray(((r[None, :] <= r[:, None]) & (r[None, :] // seg == r[:, None] // seg)).astype(np.float32), BF16)


def _placement_matrix():
    m = np.zeros((3 * LANES, 2 * LANES), np.float32)
    for part in range(3):
        for head in range(FOX_HEADS):
            m[part * LANES + EX_FORGET_LANE + head, AUG_GROUP * head + part] = 1.0
            m[part * LANES + EX_FORGET_LANE + head, LANES + AUG_GROUP * head + 3 + part] = 1.0
    return jnp.asarray(m, BF16)


def _project(x, g, w_all, wg, bg, bf, kv_prev, *, e, n_even, seq_rows, row_tile, act_dtype, with_aug):
    n, d = x.shape
    d_qk = 2 * wg.shape[1]
    tiles_per_seq = max(seq_rows // row_tile, 1)
    tseg = _segment_tril(row_tile, min(seq_rows, row_tile))
    place = _placement_matrix()
    row = lambda width: pl.BlockSpec((row_tile, width), lambda i: (i, 0))
    kv_shape = jax.ShapeDtypeStruct((n_even, n * FOX_HEADS, LANES), F32)
    kv_spec = pl.BlockSpec((None, row_tile * FOX_HEADS, LANES), lambda i: (e, i, 0))
    out_shape = [
        jax.ShapeDtypeStruct((n, d_qk), act_dtype), jax.ShapeDtypeStruct((n, 512), act_dtype),
        jax.ShapeDtypeStruct((n, 512), act_dtype), jax.ShapeDtypeStruct((n, 512), act_dtype),
        kv_shape, kv_shape,
        jax.ShapeDtypeStruct((n, d_qk // 2), F32), jax.ShapeDtypeStruct((n, LANES), F32),
        jax.ShapeDtypeStruct((n, LANES), F32)]
    out_specs = [row(d_qk), row(512), row(512), row(512), kv_spec, kv_spec, row(d_qk // 2),
                 row(LANES), row(LANES)]
    if with_aug:
        out_shape += [jax.ShapeDtypeStruct((n, LANES), BF16)] * 2
        out_specs += [row(LANES), row(LANES)]
    fox_scale = float(LANES) ** -0.5 * (LOG2E if with_aug else 1.0)
    kern = functools.partial(_proj_kernel, tiles_per_seq=tiles_per_seq, fox_scale=fox_scale,
                             with_aug=with_aug, has_prev=kv_prev is not None)
    args = [x, g, w_all, wg, bg, bf, tseg, place]
    in_specs = [row(d), _const_spec(g.shape),
                pl.BlockSpec((None,) + w_all.shape[1:], lambda i: (e, 0, 0), pipeline_mode=pl.Buffered(1)),
                _const_spec(wg.shape), _const_spec(bg.shape), _const_spec(bf.shape),
                _const_spec(tseg.shape), _const_spec(place.shape)]
    aliases = {}
    if kv_prev is not None:
        aliases = {len(args): 4, len(args) + 1: 5}
        args += list(kv_prev)
        in_specs += [pl.BlockSpec(memory_space=pl.ANY)] * 2
    return pl.pallas_call(
        kern, grid=(n // row_tile,), in_specs=in_specs, out_specs=out_specs, out_shape=out_shape,
        input_output_aliases=aliases,
        scratch_shapes=[pltpu.VMEM((SUBLANES, LANES), F32)],
        compiler_params=_params("arbitrary"), name="even_project",
    )(*args)


def _gla_kernel(q_ref, k_ref, la_ref, v_ref, g_ref, s0_ref, gn_ref, tril_ref, ind_ref,
                o_ref, sn_ref, s_ref, a_ref, kk_ref, bb_ref, *, rows, real_rows):
    C = GLA_CHUNK
    dk = s0_ref.shape[1]
    dv = s0_ref.shape[2]
    n_chunks = max(rows // C, 1)

    s_ref[...] = jnp.zeros_like(s_ref)
    s_ref[0:dk, 0:dv] = s0_ref[0]
    s_ref[dk:2 * dk, dv:2 * dv] = s0_ref[1]

    lane = lax.broadcasted_iota(jnp.int32, (C, 2 * dk), 1)
    rowi = lax.broadcasted_iota(jnp.int32, (C, C), 0)
    coli = lax.broadcasted_iota(jnp.int32, (C, C), 1)
    causal = coli <= rowi
    sr = lax.broadcasted_iota(jnp.int32, (2 * dk, 2 * dv), 0)
    sc = lax.broadcasted_iota(jnp.int32, (2 * dk, 2 * dv), 1)
    same_head = (sr // dk) == (sc // dv)

    def load(ref, c, dtype):
        if rows >= C:
            return ref[pl.ds(pl.multiple_of(c * C, C), C), :].astype(dtype)
        x = ref[...].astype(dtype)
        return jnp.concatenate([x, jnp.zeros((C - rows, x.shape[1]), dtype)], axis=0)

    def chunk(c, carry):
        q = load(q_ref, c, F32)
        k = load(k_ref, c, F32)
        la = load(la_ref, c, F32)
        if real_rows < rows or rows < C:
            real = lax.broadcasted_iota(jnp.int32, la.shape, 0) < real_rows
            la = jnp.where(real, la, 0.0)
            k = jnp.where(real, k, 0.0)
        v = load(v_ref, c, BF16)
        b = _dot_left01(tril_ref[...], la)
        b_last = b[C - 1:C, :]
        q_in = q * jnp.exp(b)
        k_hat = (k * jnp.exp(b_last - b)).astype(BF16)
        safe = jnp.min(b_last) >= GLA_SAFE_LOG_DECAY

        @pl.when(safe)
        def _():
            k_out = (k * jnp.exp(-b)).astype(BF16)
            for hh in range(2):
                qm = jnp.where((lane // dk) == hh, q_in, 0.0).astype(BF16)
                a_ref[hh * C:(hh + 1) * C, :] = _dot_nt(qm, k_out)

        @pl.when(jnp.logical_not(safe))
        def _():
            kk_ref[...] = k
            bb_ref[...] = b
            a_ref[...] = jnp.zeros_like(a_ref)

            def key_row(s, carry2):
                ks = kk_ref[pl.ds(s, 1), :]
                bs = bb_ref[pl.ds(s, 1), :]
                e = q * jnp.exp(jnp.minimum(b - bs, 0.0)) * ks
                sums = _dot(e.astype(BF16), ind_ref[...])
                hit = coli == s
                a_ref[0:C, :] += jnp.where(hit, sums[:, 0:1], 0.0)
                a_ref[C:2 * C, :] += jnp.where(hit, sums[:, 1:2], 0.0)
                return carry2

            lax.fori_loop(0, C, key_row, 0)

        s_bd = s_ref[...]
        o_inter = _dot(q_in.astype(BF16), s_bd.astype(BF16))
        gn = gn_ref[...]
        g = load(g_ref, c, F32)
        outs = []
        for hh in range(2):
            a = jnp.where(causal, a_ref[hh * C:(hh + 1) * C, :], 0.0).astype(BF16)
            o = _dot(a, v[:, hh * dv:(hh + 1) * dv]) + o_inter[:, hh * dv:(hh + 1) * dv]
            gh = g[:, hh * dv:(hh + 1) * dv]
            outs.append(o * _rms_scale(o) * gn * (gh * jax.nn.sigmoid(gh)))
        o_full = jnp.concatenate(outs, axis=1)
        if rows >= C:
            o_ref[pl.ds(pl.multiple_of(c * C, C), C), :] = o_full.astype(o_ref.dtype)
        else:
            o_ref[...] = o_full[0:rows, :].astype(o_ref.dtype)

        upd = _dot_tn(k_hat, v)
        decay = jnp.exp(jnp.broadcast_to(b_last, (2 * dk, 2 * dk)).T)
        decay2 = jnp.concatenate([decay] * (2 * dv // (2 * dk)), axis=1)
        s_ref[...] = jnp.where(same_head, s_bd * decay2 + upd, 0.0)
        return carry

    lax.fori_loop(0, n_chunks, chunk, 0)
    sn_ref[0] = s_ref[0:dk, 0:dv]
    sn_ref[1] = s_ref[dk:2 * dk, dv:2 * dv]


def _gla(qk, la, va, ga, s0, gn, *, n_seq, rows, real_rows, out_dtype):
    n = qk.shape[0]
    dk, dv = s0.shape[2], s0.shape[3]
    pairs = GLA_HEADS // 2
    C = GLA_CHUNK
    tril = _segment_tril(C, C)
    ind = np.zeros((2 * dk, LANES), np.float32)
    ind[:dk, 0] = 1.0
    ind[dk:, 1] = 1.0
    ind = jnp.asarray(ind, BF16)
    kern = functools.partial(_gla_kernel, rows=rows, real_rows=real_rows)
    o, sn = pl.pallas_call(
        kern, grid=(n_seq, pairs),
        in_specs=[pl.BlockSpec((rows, 2 * dk), lambda b, p: (b, p)),
                  pl.BlockSpec((rows, 2 * dk), lambda b, p: (b, pairs + p)),
                  pl.BlockSpec((rows, 2 * dk), lambda b, p: (b, p)),
                  pl.BlockSpec((rows, 2 * dv), lambda b, p: (b, p)),
                  pl.BlockSpec((rows, 2 * dv), lambda b, p: (b, p)),
                  pl.BlockSpec((None, 2, dk, dv), lambda b, p: (b, p, 0, 0)),
                  _const_spec(gn.shape), _const_spec(tril.shape), _const_spec(ind.shape)],
        out_specs=[pl.BlockSpec((rows, 2 * dv), lambda b, p: (b, p)),
                   pl.BlockSpec((None, 2, dk, dv), lambda b, p: (b, p, 0, 0))],
        out_shape=[jax.ShapeDtypeStruct((n, GLA_HEADS * dv), out_dtype),
                   jax.ShapeDtypeStruct(s0.shape, F32)],
        scratch_shapes=[pltpu.VMEM((2 * dk, 2 * dv), F32), pltpu.VMEM((2 * C, C), F32),
                        pltpu.VMEM((C, 2 * dk), F32), pltpu.VMEM((C, 2 * dk), F32)],
        compiler_params=_params("arbitrary", "arbitrary"), name="gla_scan",
    )(qk, qk, la, va, ga, s0, gn, tril, ind)
    return o, sn


def _fox_prompt_kernel(q_ref, qa_ref, k_ref, v_ref, ka_ref, o_ref, *, tile):
    head = pl.program_id(1)
    nt = q_ref.shape[0] // tile
    lane = lax.broadcasted_iota(jnp.int32, (tile, LANES), 1)
    own = (lane // AUG_GROUP) == head
    r = lax.broadcasted_iota(jnp.int32, (tile, tile), 0)
    c = lax.broadcasted_iota(jnp.int32, (tile, tile), 1)
    causal = c <= r
    qf = []
    for i in range(nt):
        qa = qa_ref[i * tile:(i + 1) * tile, :]
        qf.append(jnp.concatenate([q_ref[i * tile:(i + 1) * tile, :], jnp.where(own, qa, jnp.zeros_like(qa))],
                                  axis=1))
    m, l, acc = [None] * nt, [None] * nt, [None] * nt
    for j in range(nt):
        key_rows = pl.ds(j * tile * FOX_HEADS + head, tile, stride=FOX_HEADS)
        kf = jnp.concatenate([k_ref[key_rows, :].astype(BF16), ka_ref[j * tile:(j + 1) * tile, :]], axis=1)
        vj = v_ref[key_rows, :].astype(BF16)
        for i in range(j, nt):
            s = _dot_nt(qf[i], kf)
            if i == j:
                s = jnp.where(causal, s, -jnp.inf)
            s_max = jnp.max(s, axis=1, keepdims=True)
            if j == 0:
                m[i] = s_max
                p = jnp.exp2(s - m[i])
                l[i] = jnp.sum(p, axis=1, keepdims=True)
                acc[i] = _dot(p.astype(BF16), vj)
            else:
                m_new = jnp.maximum(m[i], s_max)
                p = jnp.exp2(s - m_new)
                alpha = jnp.exp2(m[i] - m_new)
                l[i] = alpha * l[i] + jnp.sum(p, axis=1, keepdims=True)
                acc[i] = alpha * acc[i] + _dot(p.astype(BF16), vj)
                m[i] = m_new
        o_ref[j * tile:(j + 1) * tile, :] = (acc[j] / l[j]).astype(o_ref.dtype)


def _fox_prompt(qb, qaug, k_all, v_all, kaug, *, e, n_seq, seq, out_dtype):
    n = qb.shape[0]
    return pl.pallas_call(
        functools.partial(_fox_prompt_kernel, tile=FOX_TILE), grid=(n_seq, FOX_HEADS),
        in_specs=[pl.BlockSpec((seq, LANES), lambda b, h: (b, h)),
                  pl.BlockSpec((seq, LANES), lambda b, h: (b, 0)),
                  pl.BlockSpec((None, seq * FOX_HEADS, LANES), lambda b, h: (e, b, 0)),
                  pl.BlockSpec((None, seq * FOX_HEADS, LANES), lambda b, h: (e, b, 0)),
                  pl.BlockSpec((seq, LANES), lambda b, h: (b, 0))],
        out_specs=pl.BlockSpec((seq, LANES), lambda b, h: (b, h)),
        out_shape=jax.ShapeDtypeStruct((n, FOX_HEADS * LANES), out_dtype),
        compiler_params=_params("arbitrary", "arbitrary"), name="fox_prompt",
    )(qb, qaug, k_all, v_all, kaug)


def _fox_bias_kernel(pt_ref, cache_ref, fc_ref, m12_ref, upper_ref, pm_ref, ones_ref,
                     rh_ref, cmask_ref, bnew_ref, g_ref, *, real_rows):
    b = pl.program_id(0)
    n_pages = pt_ref.shape[1]
    width = cache_ref.shape[1]

    def gather(p, carry):
        g_ref[pl.ds(p, 1), :] = cache_ref[pl.ds(pt_ref[b, p], 1), :]
        return carry

    lax.fori_loop(0, n_pages, gather, 0)
    w = _dot_right01(g_ref[...], m12_ref[...])
    rh_ref[...] = w[:, :width] + _dot_left01(upper_ref[...], w[:, width:])

    f8 = fc_ref[...]
    rows32 = FOX_HEADS * SAMPLE_PAD
    r = lax.broadcasted_iota(jnp.int32, (rows32, LANES), 0)
    c = lax.broadcasted_iota(jnp.int32, (rows32, LANES), 1)
    own = c == (EX_FORGET_LANE + r // SAMPLE_PAD)
    f_query = _dot_right01(jnp.where(own, jnp.concatenate([f8] * FOX_HEADS, axis=0), 0.0), ones_ref[...])
    r8 = lax.broadcasted_iota(jnp.int32, (SAMPLE_PAD, LANES), 0)
    c8 = lax.broadcasted_iota(jnp.int32, (SAMPLE_PAD, LANES), 1)
    f_key = jnp.sum(jnp.where(r8 == c8 // FOX_HEADS, _dot_right01(f8, pm_ref[...]), 0.0), axis=0, keepdims=True)
    key_head, key_t = c % FOX_HEADS, c // FOX_HEADS
    head, t = r // SAMPLE_PAD, r % SAMPLE_PAD
    valid = (key_head == head) & (key_t <= t) & (key_t < real_rows)
    bnew_ref[...] = jnp.where(valid, f_query - f_key, -jnp.inf)
    cw = lax.broadcasted_iota(jnp.int32, (rows32, width), 1)
    rw = lax.broadcasted_iota(jnp.int32, (rows32, width), 0)
    cmask_ref[...] = jnp.where(cw % FOX_HEADS == rw // SAMPLE_PAD,
                               jnp.concatenate([f_query] * (width // LANES), axis=1), -jnp.inf)


def _fox_bias(page_table, cache_logf, fc_s, *, real_rows):
    n_seq, n_pages = page_table.shape
    width = cache_logf.shape[1]
    i_head, i_pos = np.arange(width) // PAGE_SIZE, np.arange(width) % PAGE_SIZE
    j_pos, j_head = np.arange(width) // FOX_HEADS, np.arange(width) % FOX_HEADS
    same = i_head[:, None] == j_head[None, :]
    m12 = jnp.asarray(np.concatenate([same & (i_pos[:, None] > j_pos[None, :]), same], axis=1).astype(np.float32), BF16)
    pg = np.arange(n_pages)
    upper = jnp.asarray((pg[None, :] > pg[:, None]).astype(np.float32), BF16)
    ln = np.arange(LANES)
    pm = jnp.asarray((ln[:, None] == EX_FORGET_LANE + ln[None, :] % FOX_HEADS).astype(np.float32), BF16)
    ones = jnp.ones((LANES, LANES), BF16)
    rows32 = FOX_HEADS * SAMPLE_PAD
    const = lambda a: pl.BlockSpec(a.shape, lambda b, pt: (0,) * a.ndim, pipeline_mode=pl.Buffered(1))
    grid_spec = pltpu.PrefetchScalarGridSpec(
        num_scalar_prefetch=1, grid=(n_seq,),
        in_specs=[const(cache_logf), pl.BlockSpec((SAMPLE_PAD, LANES), lambda b, pt: (b, 0)),
                  const(m12), const(upper), const(pm), const(ones)],
        out_specs=[pl.BlockSpec((None, n_pages, width), lambda b, pt: (b, 0, 0)),
                   pl.BlockSpec((None, rows32, width), lambda b, pt: (b, 0, 0)),
                   pl.BlockSpec((None, rows32, LANES), lambda b, pt: (b, 0, 0))],
        scratch_shapes=[pltpu.VMEM((n_pages, width), F32)])
    return pl.pallas_call(
        functools.partial(_fox_bias_kernel, real_rows=real_rows), grid_spec=grid_spec,
        out_shape=[jax.ShapeDtypeStruct((n_seq, n_pages, width), F32),
                   jax.ShapeDtypeStruct((n_seq, rows32, width), F32),
                   jax.ShapeDtypeStruct((n_seq, rows32, LANES), F32)],
        compiler_params=_params("arbitrary"), name="fox_sample_bias",
    )(page_table, cache_logf, fc_s, m12, upper, pm, ones)


def _fox_sample_kernel(pt_ref, q_ref, rh_ref, cmask_ref, bnew_ref, kn_ref, vn_ref, *rest, n_group):
    k_refs = rest[:n_group]
    v_refs = rest[n_group:2 * n_group]
    o_ref, m_ref, l_ref, acc_ref = rest[2 * n_group:]
    g = pl.program_id(1)
    q8 = q_ref[...]
    q = jnp.concatenate([q8[:, hh * LANES:(hh + 1) * LANES] for hh in range(FOX_HEADS)], axis=0).astype(BF16)
    cmask = cmask_ref[...]

    @pl.when(g == 0)
    def _():
        m_ref[...] = jnp.full_like(m_ref, -jnp.inf)
        l_ref[...] = jnp.zeros_like(l_ref)
        acc_ref[...] = jnp.zeros_like(acc_ref)

    def update(scores, values):
        s = jnp.concatenate(scores, axis=1) if len(scores) > 1 else scores[0]
        m_old = m_ref[...]
        m_new = jnp.maximum(m_old, jnp.max(s, axis=1, keepdims=True))
        p32 = jnp.exp(s - m_new)
        p = p32.astype(BF16)
        alpha = jnp.exp(m_old - m_new)
        l_ref[...] = alpha * l_ref[...] + jnp.sum(p32, axis=1, keepdims=True)
        pv, off = None, 0
        for v in values:
            part = _dot(p[:, off:off + v.shape[0]], v)
            pv = part if pv is None else pv + part
            off += v.shape[0]
        acc_ref[...] = alpha * acc_ref[...] + pv
        m_ref[...] = m_new

    scores, values = [], []
    for j in range(n_group):
        scores.append(_dot_nt(q, k_refs[j][...].astype(BF16)) + rh_ref[j:j + 1, :] + cmask)
        values.append(v_refs[j][...].astype(BF16))
    update(scores, values)

    @pl.when(g == pl.num_programs(1) - 1)
    def _():
        pad = jnp.zeros((LANES - kn_ref.shape[0], kn_ref.shape[1]), BF16)
        kn = jnp.concatenate([kn_ref[...].astype(BF16), pad], axis=0)
        vn = jnp.concatenate([vn_ref[...].astype(BF16), pad], axis=0)
        update([_dot_nt(q, kn) + bnew_ref[...]], [vn])
        o = acc_ref[...] / l_ref[...]
        o_ref[...] = jnp.concatenate(
            [o[hh * SAMPLE_PAD:(hh + 1) * SAMPLE_PAD, :] for hh in range(FOX_HEADS)], axis=1).astype(o_ref.dtype)


def _fox_sample(page_table, qb_s, rh, cmask, bnew, kn, vn, cache_k, cache_v, *, layer):
    n_seq, n_pages = page_table.shape
    G = PAGES_PER_STEP
    rows32 = cmask.shape[1]
    width = qb_s.shape[1]
    page_rows, dh = cache_k.shape[2], cache_k.shape[3]

    def page_spec(j):
        return pl.BlockSpec((None, None, page_rows, dh), lambda b, g, pt: (layer, pt[b, g * G + j], 0, 0))

    grid_spec = pltpu.PrefetchScalarGridSpec(
        num_scalar_prefetch=1, grid=(n_seq, n_pages // G),
        in_specs=[pl.BlockSpec((SAMPLE_PAD, width), lambda b, g, pt: (b, 0)),
                  pl.BlockSpec((None, G, page_rows), lambda b, g, pt: (b, g, 0)),
                  pl.BlockSpec((None, rows32, page_rows), lambda b, g, pt: (b, 0, 0)),
                  pl.BlockSpec((None, rows32, LANES), lambda b, g, pt: (b, 0, 0)),
                  pl.BlockSpec((None, rows32, dh), lambda b, g, pt: (layer, b, 0)),
                  pl.BlockSpec((None, rows32, dh), lambda b, g, pt: (layer, b, 0))]
                 + [page_spec(j) for j in range(G)] + [page_spec(j) for j in range(G)],
        out_specs=pl.BlockSpec((SAMPLE_PAD, width), lambda b, g, pt: (b, 0)),
        scratch_shapes=[pltpu.VMEM((rows32, 1), F32), pltpu.VMEM((rows32, 1), F32),
                        pltpu.VMEM((rows32, dh), F32)])
    return pl.pallas_call(
        functools.partial(_fox_sample_kernel, n_group=G), grid_spec=grid_spec,
        out_shape=jax.ShapeDtypeStruct((n_seq * SAMPLE_PAD, width), F32),
        compiler_params=_params("arbitrary", "arbitrary"), name="fox_sample",
    )(page_table, qb_s, rh, cmask, bnew, kn, vn, *([cache_k] * G), *([cache_v] * G))


def _pool_kernel(x_ref, g_ref, buf_ref, w_ref, sc_ref, xo_ref, tail_ref, halo_ref, *, pos0):
    i = pl.program_id(1)
    tm = x_ref.shape[0]
    halo = halo_ref.shape[0]
    gc = w_ref.shape[1]

    @pl.when(i == 0)
    def _():
        halo_ref[...] = buf_ref[...]

    x = x_ref[...]
    h = x * _rms_scale(x) * g_ref[...]
    ext = jnp.concatenate([halo_ref[...], h], axis=0)
    tail_ref[...] = ext[tm + halo - tail_ref.shape[0]:, :]
    if tm >= halo:
        halo_ref[...] = h[tm - halo:, :]
    pos = pos0 + i * tm + lax.broadcasted_iota(jnp.int32, (tm, gc), 0) + 1
    ys = []
    for grp, win in enumerate(POOL_WINDOWS):
        acc = ext[:, grp * gc:(grp + 1) * gc]
        shift = 1
        while shift < win:
            acc = acc + pltpu.roll(acc, shift, axis=0)
            shift *= 2
        cnt = jnp.minimum(win, pos).astype(F32)
        pooled = acc[halo:, :] / cnt - h[:, grp * gc:(grp + 1) * gc]
        ys.append(_dot(pooled.astype(BF16), w_ref[grp]))
    xo_ref[...] = x + jnp.concatenate(ys, axis=1) * sc_ref[...]


def _pool(x, g, buf16, w, sc, *, n_seq, rows, row_tile, pos0):
    n, d = x.shape
    nt = rows // row_tile
    tail_rows = 3 * SUBLANES
    return pl.pallas_call(
        functools.partial(_pool_kernel, pos0=pos0), grid=(n_seq, nt),
        in_specs=[pl.BlockSpec((row_tile, d), lambda b, i: (b * nt + i, 0)),
                  _const_spec(g.shape),
                  pl.BlockSpec((None,) + buf16.shape[1:], lambda b, i: (b, 0, 0)),
                  _const_spec(w.shape), _const_spec(sc.shape)],
        out_specs=[pl.BlockSpec((row_tile, d), lambda b, i: (b * nt + i, 0)),
                   pl.BlockSpec((None, tail_rows, d), lambda b, i: (b, 0, 0))],
        out_shape=[jax.ShapeDtypeStruct((n, d), F32), jax.ShapeDtypeStruct((n_seq, tail_rows, d), F32)],
        scratch_shapes=[pltpu.VMEM(buf16.shape[1:], F32)],
        compiler_params=_params("arbitrary", "arbitrary"), name="pool_mix",
    )(x, g, buf16, w, sc)


def _mlp_kernel(*refs, with_merge, with_final):
    if with_merge:
        x_ref, oa_ref, ob_ref, wo_ref, g_ref, wu_ref, wd_ref, gf_ref, o_ref, acc_ref = refs
        half = oa_ref.shape[1]
        x = (x_ref[...] + _dot(oa_ref[...].astype(BF16), wo_ref[0:half, :])
             + _dot(ob_ref[...].astype(BF16), wo_ref[half:, :]))
    else:
        x_ref, g_ref, wu_ref, wd_ref, gf_ref, o_ref, acc_ref = refs
        x = x_ref[...]
    h = (x * _rms_scale(x) * g_ref[...]).astype(BF16)
    acc_ref[...] = x

    def ff_chunk(c, carry):
        cols = pl.ds(pl.multiple_of(c * FF_CHUNK, FF_CHUNK), FF_CHUNK)
        a = jnp.maximum(_dot(h, wu_ref[:, cols]), 0.0)
        acc_ref[...] += _dot((a * a).astype(BF16), wd_ref[cols, :])
        return carry

    lax.fori_loop(0, wu_ref.shape[1] // FF_CHUNK, ff_chunk, 0)
    y = acc_ref[...]
    if with_final:
        y = y * _rms_scale(y) * gf_ref[...]
    o_ref[...] = y


def _mlp(x, g, wu_all, wd_all, gf, *, layer, row_tile, merge=None, with_final=False):
    n, d = x.shape
    row = lambda width: pl.BlockSpec((row_tile, width), lambda i: (i, 0))
    pick = lambda a, idx: pl.BlockSpec((None,) + a.shape[1:], lambda i: (idx,) + (0,) * (a.ndim - 1),
                                       pipeline_mode=pl.Buffered(1))
    args, specs = [x], [row(d)]
    if merge is not None:
        oa, ob, wo_all, e = merge
        args += [oa, ob, wo_all]
        specs += [row(oa.shape[1]), row(ob.shape[1]), pick(wo_all, e)]
    args += [g, wu_all, wd_all, gf]
    specs += [_const_spec(g.shape), pick(wu_all, layer), pick(wd_all, layer), _const_spec(gf.shape)]
    return pl.pallas_call(
        functools.partial(_mlp_kernel, with_merge=merge is not None, with_final=with_final),
        grid=(n // row_tile,), in_specs=specs, out_specs=row(d),
        out_shape=jax.ShapeDtypeStruct((n, d), F32),
        scratch_shapes=[pltpu.VMEM((row_tile, d), F32)],
        compiler_params=_params("arbitrary"), name="mlp",
    )(*args)


def kernel(x_prompt, x_sample, cache_fox_k, cache_fox_v, cache_fox_logf, state_gla, state_pool, page_table,
           norm_mix, norm_mlp, norm_final, w_in_even, w_gate_up, b_gate, gla_norm, b_forget, w_out_even,
           w_pool, pool_scale, w_mlp_up, w_mlp_down):
    n_prompt, seq, d = x_prompt.shape
    n_sample, t_new, _ = x_sample.shape
    depth = norm_mix.shape[0]
    n_even, n_pool = cache_fox_k.shape[0], cache_fox_k.shape[1]
    fox_dh = cache_fox_k.shape[-1]
    fox_total = FOX_HEADS * fox_dh
    dk, dv = state_gla.shape[3], state_gla.shape[4]
    dk_total, dv_total = GLA_HEADS * dk, GLA_HEADS * dv
    d_ff = w_mlp_up.shape[2]
    past_len = page_table.shape[1] * PAGE_SIZE
    T8 = SAMPLE_PAD

    xp = x_prompt.reshape(n_prompt * seq, d)
    xs = jnp.pad(x_sample, ((0, 0), (0, T8 - t_new), (0, 0))).reshape(n_sample * T8, d)

    row2 = lambda v: v.reshape(1, -1).astype(F32)
    wu = w_mlp_up.astype(BF16)
    wd = w_mlp_down.astype(BF16)
    wo = w_out_even.astype(BF16)
    w_in = _prep_in_proj(w_in_even, 2 * dk_total + dv_total)
    gfin = row2(norm_final)
    kv_p = kv_s = None
    cache_k = cache_fox_k.reshape(n_even, n_pool, PAGE_SIZE * FOX_HEADS, fox_dh)
    cache_v = cache_fox_v.reshape(n_even, n_pool, PAGE_SIZE * FOX_HEADS, fox_dh)
    cache_f = cache_fox_logf.transpose(0, 1, 3, 2).reshape(n_even, n_pool, FOX_HEADS * PAGE_SIZE)

    kp, vp, fp, ks, vs, fs, gp, gs, pp, ps = ([] for _ in range(10))
    for layer in range(depth):
        g_mix = row2(norm_mix[layer])
        g_mlp = row2(norm_mlp[layer])
        last = layer == depth - 1
        if layer % 2 == 0:
            e = layer // 2
            wg = jnp.zeros((LANES, dk_total), F32).at[EX_RANK_LANE:EX_RANK_LANE + GLA_RANK].set(
                w_gate_up[e]).astype(BF16)
            bg = row2(b_gate[e])
            bf = jnp.zeros((1, LANES), F32).at[0, EX_FORGET_LANE:EX_FORGET_LANE + FOX_HEADS].set(b_forget[e])
            gn = row2(gla_norm[e])

            qk, va, ga, qb, k_all, v_all, la, lf, fc, qaug, kaug = _project(
                xp, g_mix, w_in, wg, bg, bf, kv_p, e=e, n_even=n_even, seq_rows=seq, row_tile=ROW_TILE,
                act_dtype=BF16, with_aug=True)
            kv_p = (k_all, v_all)
            s0 = jnp.zeros((n_prompt, GLA_HEADS, dk, dv), F32)
            oa, sp_new = _gla(qk, la, va, ga, s0, gn, n_seq=n_prompt, rows=seq, real_rows=seq, out_dtype=BF16)
            ob = _fox_prompt(qb, qaug, k_all, v_all, kaug, e=e, n_seq=n_prompt, seq=seq, out_dtype=BF16)
            xp = _mlp(xp, g_mlp, wu, wd, gfin, layer=layer, row_tile=ROW_TILE, merge=(oa, ob, wo, e),
                      with_final=last)
            fp.append(lf[:, EX_FORGET_LANE:EX_FORGET_LANE + FOX_HEADS].reshape(n_prompt, seq, FOX_HEADS))
            gp.append(sp_new.astype(state_gla.dtype))

            qk, va, ga, qb, k_all, v_all, la, lf, fc = _project(
                xs, g_mix, w_in, wg, bg, bf, kv_s, e=e, n_even=n_even, seq_rows=T8, row_tile=n_sample * T8,
                act_dtype=F32, with_aug=False)
            kv_s = (k_all, v_all)
            oa, ss_new = _gla(qk, la, va, ga, state_gla[e].astype(F32), gn, n_seq=n_sample, rows=T8,
                              real_rows=t_new, out_dtype=F32)
            rh, cmask, bnew = _fox_bias(page_table, cache_f[e], fc, real_rows=t_new)
            ob = _fox_sample(page_table, qb, rh, cmask, bnew, k_all, v_all, cache_k, cache_v, layer=e)
            xs = _mlp(xs, g_mlp, wu, wd, gfin, layer=layer, row_tile=n_sample * T8, merge=(oa, ob, wo, e),
                      with_final=last)
            fs.append(lf[:, EX_FORGET_LANE:EX_FORGET_LANE + FOX_HEADS].reshape(n_sample, T8, FOX_HEADS)[:, :t_new])
            gs.append(ss_new.astype(state_gla.dtype))
        else:
            o = layer // 2
            wpl = w_pool[o].astype(BF16)
            sc = row2(pool_scale[o])
            halo = POOL_BUF + 1
            xp, tail = _pool(xp, g_mix, jnp.zeros((n_prompt, halo, d), F32), wpl, sc, n_seq=n_prompt,
                             rows=seq, row_tile=ROW_TILE, pos0=0)
            pp.append(tail[:, tail.shape[1] - POOL_BUF:])
            buf16 = jnp.pad(state_pool[o].astype(F32), ((0, 0), (1, 0), (0, 0)))
            xs, tail = _pool(xs, g_mix, buf16, wpl, sc, n_seq=n_sample, rows=T8, row_tile=T8, pos0=past_len)
            end = tail.shape[1] - (T8 - t_new)
            ps.append(tail[:, end - POOL_BUF:end])
            xp = _mlp(xp, g_mlp, wu, wd, gfin, layer=layer, row_tile=ROW_TILE, with_final=last)
            xs = _mlp(xs, g_mlp, wu, wd, gfin, layer=layer, row_tile=n_sample * T8, with_final=last)

    y_prompt = xp.reshape(n_prompt, seq, d)
    y_sample = xs.reshape(n_sample, T8, d)[:, :t_new]
    kp, vp = (a.reshape(n_even, n_prompt, seq, FOX_HEADS, fox_dh) for a in kv_p)
    ks, vs = (a.reshape(n_even, n_sample, T8, FOX_HEADS, fox_dh)[:, :, :t_new] for a in kv_s)
    return (y_prompt, y_sample, kp, vp, jnp.stack(fp), ks, vs, jnp.stack(fs), jnp.stack(gp), jnp.stack(gs),
            jnp.stack(pp), jnp.stack(ps))
```

```python
import functools

import numpy as np
import jax
import jax.numpy as jnp
from jax import lax
from jax.experimental import pallas as pl
from jax.experimental.pallas import tpu as pltpu

F32 = jnp.float32
BF16 = jnp.bfloat16

GLA_HEADS = 4
GLA_RANK = 16
GLA_TAU = 16.0
FOX_HEADS = 4
POOL_WINDOWS = (2, 4, 8, 16)
POOL_BUF = 15
PAGE_SIZE = 128
EPS = 1e-6

LANES = 128
SUBLANES = 8
VMEM_LIMIT_BYTES = 56 * 1024 * 1024

SAMPLE_PAD = SUBLANES
GLA_CHUNK = 128
GLA_SAFE_LOG_DECAY = -60.0
EX_RANK_LANE = 0
EX_FORGET_LANE = LANES - FOX_HEADS
LOG2E = 1.4426950408889634
FF_CHUNK = 512
AUG_GROUP = 8
ROW_TILE = 512
FOX_TILE = 512
PAGES_PER_STEP = 16


def _dot(a, b):
    return jnp.dot(a, b, preferred_element_type=F32)


def _dot_nt(a, b):
    return lax.dot_general(a, b, (((1,), (1,)), ((), ())), preferred_element_type=F32)


def _dot_tn(a, b):
    return lax.dot_general(a, b, (((0,), (0,)), ((), ())), preferred_element_type=F32)


def _split3(x):
    hi = x.astype(BF16)
    r1 = x - hi.astype(F32)
    mid = r1.astype(BF16)
    lo = (r1 - mid.astype(F32)).astype(BF16)
    return hi, mid, lo


def _dot_left01(m01, x):
    hi, mid, lo = _split3(x)
    return _dot(m01, hi) + _dot(m01, mid) + _dot(m01, lo)


def _dot_right01(x, m01):
    hi, mid, lo = _split3(x)
    return _dot(hi, m01) + _dot(mid, m01) + _dot(lo, m01)


def _log_sigmoid(x):
    return jnp.minimum(x, 0.0) - jnp.log1p(jnp.exp(-jnp.abs(x)))


def _rms_scale(x):
    return lax.rsqrt(jnp.mean(x * x, axis=-1, keepdims=True) + EPS)


def _params(*semantics):
    return pltpu.CompilerParams(dimension_semantics=semantics, vmem_limit_bytes=VMEM_LIMIT_BYTES)


def _const_spec(shape):
    nd = len(shape)
    return pl.BlockSpec(shape, lambda *_: (0,) * nd, pipeline_mode=pl.Buffered(1))


def _prep_in_proj_kernel(w_ref, o_ref, *, rank_off, rank, n_forget):
    x = w_ref[...]
    total = x.shape[1]
    main = jnp.concatenate([x[:, 0:rank_off], x[:, rank_off + rank:total - n_forget]], axis=1)
    lane = lax.broadcasted_iota(jnp.int32, (x.shape[0], LANES), 1)
    extras = jnp.where(lane < rank, x[:, rank_off:rank_off + LANES],
                       jnp.where(lane >= LANES - n_forget, x[:, total - LANES:total], 0.0))
    o_ref[:, 0:main.shape[1]] = main.astype(o_ref.dtype)
    o_ref[:, main.shape[1]:] = extras.astype(o_ref.dtype)


def _prep_in_proj(w_in, rank_off):
    n_even, d, total = w_in.shape
    out_cols = total - GLA_RANK - FOX_HEADS + LANES
    tile = 256
    return pl.pallas_call(
        functools.partial(_prep_in_proj_kernel, rank_off=rank_off, rank=GLA_RANK, n_forget=FOX_HEADS),
        grid=(n_even, d // tile),
        in_specs=[pl.BlockSpec((None, tile, total), lambda e, i: (e, i, 0))],
        out_specs=pl.BlockSpec((None, tile, out_cols), lambda e, i: (e, i, 0)),
        out_shape=jax.ShapeDtypeStruct((n_even, d, out_cols), BF16),
        compiler_params=_params("arbitrary", "arbitrary"), name="prep_in_proj",
    )(w_in)


def _proj_kernel(x_ref, g_ref, w_ref, wg_ref, bg_ref, bf_ref, tseg_ref, place_ref, *rest,
                 tiles_per_seq, fox_scale, with_aug, has_prev):
    if has_prev:
        rest = rest[2:]
    qk_ref, va_ref, ga_ref, qb_ref, kb_ref, vb_ref, la_ref, lf_ref, fc_ref = rest[:9]
    if with_aug:
        qaug_ref, kaug_ref, carry_ref = rest[9:]
    else:
        (carry_ref,) = rest[9:]
    i = pl.program_id(0)
    x = x_ref[...]
    h = (x * _rms_scale(x) * g_ref[...]).astype(BF16)
    d_qk = qk_ref.shape[1]

    y = _dot(h, w_ref[:, 0:d_qk])
    lane = lax.broadcasted_iota(jnp.int32, y.shape, 1)
    gla_scale = (d_qk // 2 // GLA_HEADS) ** -0.5
    qk_ref[...] = (y * jnp.where(lane < d_qk // 2, gla_scale, 1.0)).astype(qk_ref.dtype)
    c0 = d_qk
    va_ref[...] = _dot(h, w_ref[:, c0:c0 + 512]).astype(va_ref.dtype)
    ga_ref[...] = _dot(h, w_ref[:, c0 + 512:c0 + 1024]).astype(ga_ref.dtype)
    qb_ref[...] = (_dot(h, w_ref[:, c0 + 1024:c0 + 1536]) * fox_scale).astype(qb_ref.dtype)
    tm = x.shape[0]
    for out_ref, col in ((kb_ref, c0 + 1536), (vb_ref, c0 + 2048)):
        y = _dot(h, w_ref[:, col:col + 512])
        for hh in range(FOX_HEADS):
            out_ref[pl.ds(hh, tm, stride=FOX_HEADS), :] = y[:, hh * LANES:(hh + 1) * LANES]
    ex = _dot(h, w_ref[:, c0 + 2560:c0 + 2560 + LANES])

    gate = _dot(ex.astype(BF16), wg_ref[...]) + bg_ref[...]
    la_ref[...] = _log_sigmoid(gate) * (1.0 / GLA_TAU)
    lf = _log_sigmoid(ex + bf_ref[...])
    lf_ref[...] = lf

    if tiles_per_seq > 1:
        @pl.when(i % tiles_per_seq == 0)
        def _():
            carry_ref[...] = jnp.zeros_like(carry_ref)
        fc = _dot_left01(tseg_ref[...], lf) + carry_ref[0:1, :]
        carry_ref[0:1, :] = fc[fc.shape[0] - 1:, :]
    else:
        fc = _dot_left01(tseg_ref[...], lf)
    fc_ref[...] = fc

    if with_aug:
        hi, mid, lo = _split3(fc * LOG2E)
        placed = _dot(jnp.concatenate([hi, mid, lo], axis=1), place_ref[...])
        sub = lax.broadcasted_iota(jnp.int32, (x.shape[0], LANES), 1) % AUG_GROUP
        qaug_ref[...] = (placed[:, :LANES] + jnp.where((sub >= 3) & (sub < 6), 1.0, 0.0)).astype(BF16)
        kaug_ref[...] = (jnp.where(sub < 3, 1.0, 0.0) - placed[:, LANES:]).astype(BF16)


def _segment_tril(tile, seg):
    r = np.arange(tile)
    return jnp.asarray(((r[None, :] <= r[:, None]) & (r[None, :] // seg == r[:, None] // seg)).astype(np.float32), BF16)


def _placement_matrix():
    m = np.zeros((3 * LANES, 2 * LANES), np.float32)
    for part in range(3):
        for head in range(FOX_HEADS):
            m[part * LANES + EX_FORGET_LANE + head, AUG_GROUP * head + part] = 1.0
            m[part * LANES + EX_FORGET_LANE + head, LANES + AUG_GROUP * head + 3 + part] = 1.0
    return jnp.asarray(m, BF16)


def _project(x, g, w_all, wg, bg, bf, kv_prev, *, e, n_even, seq_rows, row_tile, act_dtype, with_aug):
    n, d = x.shape
    d_qk = 2 * wg.shape[1]
    tiles_per_seq = max(seq_rows // row_tile, 1)
    tseg = _segment_tril(row_tile, min(seq_rows, row_tile))
    place = _placement_matrix()
    row = lambda width: pl.BlockSpec((row_tile, width), lambda i: (i, 0))
    kv_shape = jax.ShapeDtypeStruct((n_even, n * FOX_HEADS, LANES), F32)
    kv_spec = pl.BlockSpec((None, row_tile * FOX_HEADS, LANES), lambda i: (e, i, 0))
    out_shape = [
        jax.ShapeDtypeStruct((n, d_qk), act_dtype), jax.ShapeDtypeStruct((n, 512), act_dtype),
        jax.ShapeDtypeStruct((n, 512), act_dtype), jax.ShapeDtypeStruct((n, 512), act_dtype),
        kv_shape, kv_shape,
        jax.ShapeDtypeStruct((n, d_qk // 2), F32), jax.ShapeDtypeStruct((n, LANES), F32),
        jax.ShapeDtypeStruct((n, LANES), F32)]
    out_specs = [row(d_qk), row(512), row(512), row(512), kv_spec, kv_spec, row(d_qk // 2),
                 row(LANES), row(LANES)]
    if with_aug:
        out_shape += [jax.ShapeDtypeStruct((n, LANES), BF16)] * 2
        out_specs += [row(LANES), row(LANES)]
    fox_scale = float(LANES) ** -0.5 * (LOG2E if with_aug else 1.0)
    kern = functools.partial(_proj_kernel, tiles_per_seq=tiles_per_seq, fox_scale=fox_scale,
                             with_aug=with_aug, has_prev=kv_prev is not None)
    args = [x, g, w_all, wg, bg, bf, tseg, place]
    in_specs = [row(d), _const_spec(g.shape),
                pl.BlockSpec((None,) + w_all.shape[1:], lambda i: (e, 0, 0), pipeline_mode=pl.Buffered(1)),
                _const_spec(wg.shape), _const_spec(bg.shape), _const_spec(bf.shape),
                _const_spec(tseg.shape), _const_spec(place.shape)]
    aliases = {}
    if kv_prev is not None:
        aliases = {len(args): 4, len(args) + 1: 5}
        args += list(kv_prev)
        in_specs += [pl.BlockSpec(memory_space=pl.ANY)] * 2
    return pl.pallas_call(
        kern, grid=(n // row_tile,), in_specs=in_specs, out_specs=out_specs, out_shape=out_shape,
        input_output_aliases=aliases,
        scratch_shapes=[pltpu.VMEM((SUBLANES, LANES), F32)],
        compiler_params=_params("arbitrary"), name="even_project",
    )(*args)


def _gla_kernel(q_ref, k_ref, la_ref, v_ref, g_ref, s0_ref, gn_ref, tril_ref, ind_ref,
                o_ref, sn_ref, s_ref, a_ref, kk_ref, bb_ref, *, rows, real_rows):
    C = GLA_CHUNK
    dk = s0_ref.shape[1]
    dv = s0_ref.shape[2]
    n_chunks = max(rows // C, 1)

    s_ref[...] = jnp.zeros_like(s_ref)
    s_ref[0:dk, 0:dv] = s0_ref[0]
    s_ref[dk:2 * dk, dv:2 * dv] = s0_ref[1]

    lane = lax.broadcasted_iota(jnp.int32, (C, 2 * dk), 1)
    rowi = lax.broadcasted_iota(jnp.int32, (C, C), 0)
    coli = lax.broadcasted_iota(jnp.int32, (C, C), 1)
    causal = coli <= rowi
    sr = lax.broadcasted_iota(jnp.int32, (2 * dk, 2 * dv), 0)
    sc = lax.broadcasted_iota(jnp.int32, (2 * dk, 2 * dv), 1)
    same_head = (sr // dk) == (sc // dv)

    def load(ref, c, dtype):
        if rows >= C:
            return ref[pl.ds(pl.multiple_of(c * C, C), C), :].astype(dtype)
        x = ref[...].astype(dtype)
        return jnp.concatenate([x, jnp.zeros((C - rows, x.shape[1]), dtype)], axis=0)

    def chunk(c, carry):
        q = load(q_ref, c, F32)
        k = load(k_ref, c, F32)
        la = load(la_ref, c, F32)
        if real_rows < rows or rows < C:
            real = lax.broadcasted_iota(jnp.int32, la.shape, 0) < real_rows
            la = jnp.where(real, la, 0.0)
            k = jnp.where(real, k, 0.0)
        v = load(v_ref, c, BF16)
        b = _dot_left01(tril_ref[...], la)
        b_last = b[C - 1:C, :]
        q_in = q * jnp.exp(b)
        k_hat = (k * jnp.exp(b_last - b)).astype(BF16)
        safe = jnp.min(b_last) >= GLA_SAFE_LOG_DECAY

        @pl.when(safe)
        def _():
            k_out = (k * jnp.exp(-b)).astype(BF16)
            for hh in range(2):
                qm = jnp.where((lane // dk) == hh, q_in, 0.0).astype(BF16)
                a_ref[hh * C:(hh + 1) * C, :] = _dot_nt(qm, k_out)

        @pl.when(jnp.logical_not(safe))
        def _():
            kk_ref[...] = k
            bb_ref[...] = b
            a_ref[...] = jnp.zeros_like(a_ref)

            def key_row(s, carry2):
                ks = kk_ref[pl.ds(s, 1), :]
                bs = bb_ref[pl.ds(s, 1), :]
                e = q * jnp.exp(jnp.minimum(b - bs, 0.0)) * ks
                sums = _dot(e.astype(BF16), ind_ref[...])
                hit = coli == s
                a_ref[0:C, :] += jnp.where(hit, sums[:, 0:1], 0.0)
                a_ref[C:2 * C, :] += jnp.where(hit, sums[:, 1:2], 0.0)
                return carry2

            lax.fori_loop(0, C, key_row, 0)

        s_bd = s_ref[...]
        o_inter = _dot(q_in.astype(BF16), s_bd.astype(BF16))
        gn = gn_ref[...]
        g = load(g_ref, c, F32)
        outs = []
        for hh in range(2):
            a = jnp.where(causal, a_ref[hh * C:(hh + 1) * C, :], 0.0).astype(BF16)
            o = _dot(a, v[:, hh * dv:(hh + 1) * dv]) + o_inter[:, hh * dv:(hh + 1) * dv]
            gh = g[:, hh * dv:(hh + 1) * dv]
            outs.append(o * _rms_scale(o) * gn * (gh * jax.nn.sigmoid(gh)))
        o_full = jnp.concatenate(outs, axis=1)
        if rows >= C:
            o_ref[pl.ds(pl.multiple_of(c * C, C), C), :] = o_full.astype(o_ref.dtype)
        else:
            o_ref[...] = o_full[0:rows, :].astype(o_ref.dtype)

        upd = _dot_tn(k_hat, v)
        decay = jnp.exp(jnp.broadcast_to(b_last, (2 * dk, 2 * dk)).T)
        decay2 = jnp.concatenate([decay] * (2 * dv // (2 * dk)), axis=1)
        s_ref[...] = jnp.where(same_head, s_bd * decay2 + upd, 0.0)
        return carry

    lax.fori_loop(0, n_chunks, chunk, 0)
    sn_ref[0] = s_ref[0:dk, 0:dv]
    sn_ref[1] = s_ref[dk:2 * dk, dv:2 * dv]


def _gla(qk, la, va, ga, s0, gn, *, n_seq, rows, real_rows, out_dtype):
    n = qk.shape[0]
    dk, dv = s0.shape[2], s0.shape[3]
    pairs = GLA_HEADS // 2
    C = GLA_CHUNK
    tril = _segment_tril(C, C)
    ind = np.zeros((2 * dk, LANES), np.float32)
    ind[:dk, 0] = 1.0
    ind[dk:, 1] = 1.0
    ind = jnp.asarray(ind, BF16)
    kern = functools.partial(_gla_kernel, rows=rows, real_rows=real_rows)
    o, sn = pl.pallas_call(
        kern, grid=(n_seq, pairs),
        in_specs=[pl.BlockSpec((rows, 2 * dk), lambda b, p: (b, p)),
                  pl.BlockSpec((rows, 2 * dk), lambda b, p: (b, pairs + p)),
                  pl.BlockSpec((rows, 2 * dk), lambda b, p: (b, p)),
                  pl.BlockSpec((rows, 2 * dv), lambda b, p: (b, p)),
                  pl.BlockSpec((rows, 2 * dv), lambda b, p: (b, p)),
                  pl.BlockSpec((None, 2, dk, dv), lambda b, p: (b, p, 0, 0)),
                  _const_spec(gn.shape), _const_spec(tril.shape), _const_spec(ind.shape)],
        out_specs=[pl.BlockSpec((rows, 2 * dv), lambda b, p: (b, p)),
                   pl.BlockSpec((None, 2, dk, dv), lambda b, p: (b, p, 0, 0))],
        out_shape=[jax.ShapeDtypeStruct((n, GLA_HEADS * dv), out_dtype),
                   jax.ShapeDtypeStruct(s0.shape, F32)],
        scratch_shapes=[pltpu.VMEM((2 * dk, 2 * dv), F32), pltpu.VMEM((2 * C, C), F32),
                        pltpu.VMEM((C, 2 * dk), F32), pltpu.VMEM((C, 2 * dk), F32)],
        compiler_params=_params("arbitrary", "arbitrary"), name="gla_scan",
    )(qk, qk, la, va, ga, s0, gn, tril, ind)
    return o, sn


def _fox_prompt_kernel(q_ref, qa_ref, k_ref, v_ref, ka_ref, o_ref, *, tile):
    head = pl.program_id(1)
    nt = q_ref.shape[0] // tile
    lane = lax.broadcasted_iota(jnp.int32, (tile, LANES), 1)
    own = (lane // AUG_GROUP) == head
    r = lax.broadcasted_iota(jnp.int32, (tile, tile), 0)
    c = lax.broadcasted_iota(jnp.int32, (tile, tile), 1)
    causal = c <= r
    qf = []
    for i in range(nt):
        qa = qa_ref[i * tile:(i + 1) * tile, :]
        qf.append(jnp.concatenate([q_ref[i * tile:(i + 1) * tile, :], jnp.where(own, qa, jnp.zeros_like(qa))],
                                  axis=1))
    m, l, acc = [None] * nt, [None] * nt, [None] * nt
    for j in range(nt):
        key_rows = pl.ds(j * tile * FOX_HEADS + head, tile, stride=FOX_HEADS)
        kf = jnp.concatenate([k_ref[key_rows, :].astype(BF16), ka_ref[j * tile:(j + 1) * tile, :]], axis=1)
        vj = v_ref[key_rows, :].astype(BF16)
        for i in range(j, nt):
            s = _dot_nt(qf[i], kf)
            if i == j:
                s = jnp.where(causal, s, -jnp.inf)
            s_max = jnp.max(s, axis=1, keepdims=True)
            if j == 0:
                m[i] = s_max
                p = jnp.exp2(s - m[i])
                l[i] = jnp.sum(p, axis=1, keepdims=True)
                acc[i] = _dot(p.astype(BF16), vj)
            else:
                m_new = jnp.maximum(m[i], s_max)
                p = jnp.exp2(s - m_new)
                alpha = jnp.exp2(m[i] - m_new)
                l[i] = alpha * l[i] + jnp.sum(p, axis=1, keepdims=True)
                acc[i] = alpha * acc[i] + _dot(p.astype(BF16), vj)
                m[i] = m_new
        o_ref[j * tile:(j + 1) * tile, :] = (acc[j] / l[j]).astype(o_ref.dtype)


def _fox_prompt(qb, qaug, k_all, v_all, kaug, *, e, n_seq, seq, out_dtype):
    n = qb.shape[0]
    return pl.pallas_call(
        functools.partial(_fox_prompt_kernel, tile=FOX_TILE), grid=(n_seq, FOX_HEADS),
        in_specs=[pl.BlockSpec((seq, LANES), lambda b, h: (b, h)),
                  pl.BlockSpec((seq, LANES), lambda b, h: (b, 0)),
                  pl.BlockSpec((None, seq * FOX_HEADS, LANES), lambda b, h: (e, b, 0)),
                  pl.BlockSpec((None, seq * FOX_HEADS, LANES), lambda b, h: (e, b, 0)),
                  pl.BlockSpec((seq, LANES), lambda b, h: (b, 0))],
        out_specs=pl.BlockSpec((seq, LANES), lambda b, h: (b, h)),
        out_shape=jax.ShapeDtypeStruct((n, FOX_HEADS * LANES), out_dtype),
        compiler_params=_params("arbitrary", "arbitrary"), name="fox_prompt",
    )(qb, qaug, k_all, v_all, kaug)


def _fox_bias_kernel(pt_ref, cache_ref, fc_ref, m12_ref, upper_ref, pm_ref, ones_ref,
                     rh_ref, cmask_ref, bnew_ref, g_ref, *, real_rows):
    b = pl.program_id(0)
    n_pages = pt_ref.shape[1]
    width = cache_ref.shape[1]

    def gather(p, carry):
        g_ref[pl.ds(p, 1), :] = cache_ref[pl.ds(pt_ref[b, p], 1), :]
        return carry

    lax.fori_loop(0, n_pages, gather, 0)
    w = _dot_right01(g_ref[...], m12_ref[...])
    rh_ref[...] = w[:, :width] + _dot_left01(upper_ref[...], w[:, width:])

    f8 = fc_ref[...]
    rows32 = FOX_HEADS * SAMPLE_PAD
    r = lax.broadcasted_iota(jnp.int32, (rows32, LANES), 0)
    c = lax.broadcasted_iota(jnp.int32, (rows32, LANES), 1)
    own = c == (EX_FORGET_LANE + r // SAMPLE_PAD)
    f_query = _dot_right01(jnp.where(own, jnp.concatenate([f8] * FOX_HEADS, axis=0), 0.0), ones_ref[...])
    r8 = lax.broadcasted_iota(jnp.int32, (SAMPLE_PAD, LANES), 0)
    c8 = lax.broadcasted_iota(jnp.int32, (SAMPLE_PAD, LANES), 1)
    f_key = jnp.sum(jnp.where(r8 == c8 // FOX_HEADS, _dot_right01(f8, pm_ref[...]), 0.0), axis=0, keepdims=True)
    key_head, key_t = c % FOX_HEADS, c // FOX_HEADS
    head, t = r // SAMPLE_PAD, r % SAMPLE_PAD
    valid = (key_head == head) & (key_t <= t) & (key_t < real_rows)
    bnew_ref[...] = jnp.where(valid, f_query - f_key, -jnp.inf)
    cw = lax.broadcasted_iota(jnp.int32, (rows32, width), 1)
    rw = lax.broadcasted_iota(jnp.int32, (rows32, width), 0)
    cmask_ref[...] = jnp.where(cw % FOX_HEADS == rw // SAMPLE_PAD,
                               jnp.concatenate([f_query] * (width // LANES), axis=1), -jnp.inf)


def _fox_bias(page_table, cache_logf, fc_s, *, real_rows):
    n_seq, n_pages = page_table.shape
    width = cache_logf.shape[1]
    i_head, i_pos = np.arange(width) // PAGE_SIZE, np.arange(width) % PAGE_SIZE
    j_pos, j_head = np.arange(width) // FOX_HEADS, np.arange(width) % FOX_HEADS
    same = i_head[:, None] == j_head[None, :]
    m12 = jnp.asarray(np.concatenate([same & (i_pos[:, None] > j_pos[None, :]), same], axis=1).astype(np.float32), BF16)
    pg = np.arange(n_pages)
    upper = jnp.asarray((pg[None, :] > pg[:, None]).astype(np.float32), BF16)
    ln = np.arange(LANES)
    pm = jnp.asarray((ln[:, None] == EX_FORGET_LANE + ln[None, :] % FOX_HEADS).astype(np.float32), BF16)
    ones = jnp.ones((LANES, LANES), BF16)
    rows32 = FOX_HEADS * SAMPLE_PAD
    const = lambda a: pl.BlockSpec(a.shape, lambda b, pt: (0,) * a.ndim, pipeline_mode=pl.Buffered(1))
    grid_spec = pltpu.PrefetchScalarGridSpec(
        num_scalar_prefetch=1, grid=(n_seq,),
        in_specs=[const(cache_logf), pl.BlockSpec((SAMPLE_PAD, LANES), lambda b, pt: (b, 0)),
                  const(m12), const(upper), const(pm), const(ones)],
        out_specs=[pl.BlockSpec((None, n_pages, width), lambda b, pt: (b, 0, 0)),
                   pl.BlockSpec((None, rows32, width), lambda b, pt: (b, 0, 0)),
                   pl.BlockSpec((None, rows32, LANES), lambda b, pt: (b, 0, 0))],
        scratch_shapes=[pltpu.VMEM((n_pages, width), F32)])
    return pl.pallas_call(
        functools.partial(_fox_bias_kernel, real_rows=real_rows), grid_spec=grid_spec,
        out_shape=[jax.ShapeDtypeStruct((n_seq, n_pages, width), F32),
                   jax.ShapeDtypeStruct((n_seq, rows32, width), F32),
                   jax.ShapeDtypeStruct((n_seq, rows32, LANES), F32)],
        compiler_params=_params("arbitrary"), name="fox_sample_bias",
    )(page_table, cache_logf, fc_s, m12, upper, pm, ones)


def _fox_sample_kernel(pt_ref, q_ref, rh_ref, cmask_ref, bnew_ref, kn_ref, vn_ref, *rest, n_group):
    k_refs = rest[:n_group]
    v_refs = rest[n_group:2 * n_group]
    o_ref, m_ref, l_ref, acc_ref = rest[2 * n_group:]
    g = pl.program_id(1)
    q8 = q_ref[...]
    q = jnp.concatenate([q8[:, hh * LANES:(hh + 1) * LANES] for hh in range(FOX_HEADS)], axis=0).astype(BF16)
    cmask = cmask_ref[...]

    @pl.when(g == 0)
    def _():
        m_ref[...] = jnp.full_like(m_ref, -jnp.inf)
        l_ref[...] = jnp.zeros_like(l_ref)
        acc_ref[...] = jnp.zeros_like(acc_ref)

    def update(scores, values):
        s = jnp.concatenate(scores, axis=1) if len(scores) > 1 else scores[0]
        m_old = m_ref[...]
        m_new = jnp.maximum(m_old, jnp.max(s, axis=1, keepdims=True))
        p32 = jnp.exp(s - m_new)
        p = p32.astype(BF16)
        alpha = jnp.exp(m_old - m_new)
        l_ref[...] = alpha * l_ref[...] + jnp.sum(p32, axis=1, keepdims=True)
        pv, off = None, 0
        for v in values:
            part = _dot(p[:, off:off + v.shape[0]], v)
            pv = part if pv is None else pv + part
            off += v.shape[0]
        acc_ref[...] = alpha * acc_ref[...] + pv
        m_ref[...] = m_new

    scores, values = [], []
    for j in range(n_group):
        scores.append(_dot_nt(q, k_refs[j][...].astype(BF16)) + rh_ref[j:j + 1, :] + cmask)
        values.append(v_refs[j][...].astype(BF16))
    update(scores, values)

    @pl.when(g == pl.num_programs(1) - 1)
    def _():
        pad = jnp.zeros((LANES - kn_ref.shape[0], kn_ref.shape[1]), BF16)
        kn = jnp.concatenate([kn_ref[...].astype(BF16), pad], axis=0)
        vn = jnp.concatenate([vn_ref[...].astype(BF16), pad], axis=0)
        update([_dot_nt(q, kn) + bnew_ref[...]], [vn])
        o = acc_ref[...] / l_ref[...]
        o_ref[...] = jnp.concatenate(
            [o[hh * SAMPLE_PAD:(hh + 1) * SAMPLE_PAD, :] for hh in range(FOX_HEADS)], axis=1).astype(o_ref.dtype)


def _fox_sample(page_table, qb_s, rh, cmask, bnew, kn, vn, cache_k, cache_v, *, layer):
    n_seq, n_pages = page_table.shape
    G = PAGES_PER_STEP
    rows32 = cmask.shape[1]
    width = qb_s.shape[1]
    page_rows, dh = cache_k.shape[2], cache_k.shape[3]

    def page_spec(j):
        return pl.BlockSpec((None, None, page_rows, dh), lambda b, g, pt: (layer, pt[b, g * G + j], 0, 0))

    grid_spec = pltpu.PrefetchScalarGridSpec(
        num_scalar_prefetch=1, grid=(n_seq, n_pages // G),
        in_specs=[pl.BlockSpec((SAMPLE_PAD, width), lambda b, g, pt: (b, 0)),
                  pl.BlockSpec((None, G, page_rows), lambda b, g, pt: (b, g, 0)),
                  pl.BlockSpec((None, rows32, page_rows), lambda b, g, pt: (b, 0, 0)),
                  pl.BlockSpec((None, rows32, LANES), lambda b, g, pt: (b, 0, 0)),
                  pl.BlockSpec((None, rows32, dh), lambda b, g, pt: (layer, b, 0)),
                  pl.BlockSpec((None, rows32, dh), lambda b, g, pt: (layer, b, 0))]
                 + [page_spec(j) for j in range(G)] + [page_spec(j) for j in range(G)],
        out_specs=pl.BlockSpec((SAMPLE_PAD, width), lambda b, g, pt: (b, 0)),
        scratch_shapes=[pltpu.VMEM((rows32, 1), F32), pltpu.VMEM((rows32, 1), F32),
                        pltpu.VMEM((rows32, dh), F32)])
    return pl.pallas_call(
        functools.partial(_fox_sample_kernel, n_group=G), grid_spec=grid_spec,
        out_shape=jax.ShapeDtypeStruct((n_seq * SAMPLE_PAD, width), F32),
        compiler_params=_params("arbitrary", "arbitrary"), name="fox_sample",
    )(page_table, qb_s, rh, cmask, bnew, kn, vn, *([cache_k] * G), *([cache_v] * G))


def _pool_kernel(x_ref, g_ref, buf_ref, w_ref, sc_ref, xo_ref, tail_ref, halo_ref, *, pos0):
    i = pl.program_id(1)
    tm = x_ref.shape[0]
    halo = halo_ref.shape[0]
    gc = w_ref.shape[1]

    @pl.when(i == 0)
    def _():
        halo_ref[...] = buf_ref[...]

    x = x_ref[...]
    h = x * _rms_scale(x) * g_ref[...]
    ext = jnp.concatenate([halo_ref[...], h], axis=0)
    tail_ref[...] = ext[tm + halo - tail_ref.shape[0]:, :]
    if tm >= halo:
        halo_ref[...] = h[tm - halo:, :]
    pos = pos0 + i * tm + lax.broadcasted_iota(jnp.int32, (tm, gc), 0) + 1
    ys = []
    for grp, win in enumerate(POOL_WINDOWS):
        acc = ext[:, grp * gc:(grp + 1) * gc]
        shift = 1
        while shift < win:
            acc = acc + pltpu.roll(acc, shift, axis=0)
            shift *= 2
        cnt = jnp.minimum(win, pos).astype(F32)
        pooled = acc[halo:, :] / cnt - h[:, grp * gc:(grp + 1) * gc]
        ys.append(_dot(pooled.astype(BF16), w_ref[grp]))
    xo_ref[...] = x + jnp.concatenate(ys, axis=1) * sc_ref[...]


def _pool(x, g, buf16, w, sc, *, n_seq, rows, row_tile, pos0):
    n, d = x.shape
    nt = rows // row_tile
    tail_rows = 3 * SUBLANES
    return pl.pallas_call(
        functools.partial(_pool_kernel, pos0=pos0), grid=(n_seq, nt),
        in_specs=[pl.BlockSpec((row_tile, d), lambda b, i: (b * nt + i, 0)),
                  _const_spec(g.shape),
                  pl.BlockSpec((None,) + buf16.shape[1:], lambda b, i: (b, 0, 0)),
                  _const_spec(w.shape), _const_spec(sc.shape)],
        out_specs=[pl.BlockSpec((row_tile, d), lambda b, i: (b * nt + i, 0)),
                   pl.BlockSpec((None, tail_rows, d), lambda b, i: (b, 0, 0))],
        out_shape=[jax.ShapeDtypeStruct((n, d), F32), jax.ShapeDtypeStruct((n_seq, tail_rows, d), F32)],
        scratch_shapes=[pltpu.VMEM(buf16.shape[1:], F32)],
        compiler_params=_params("arbitrary", "arbitrary"), name="pool_mix",
    )(x, g, buf16, w, sc)


def _mlp_kernel(*refs, with_merge, with_final):
    if with_merge:
        x_ref, oa_ref, ob_ref, wo_ref, g_ref, wu_ref, wd_ref, gf_ref, o_ref, acc_ref = refs
        half = oa_ref.shape[1]
        x = (x_ref[...] + _dot(oa_ref[...].astype(BF16), wo_ref[0:half, :])
             + _dot(ob_ref[...].astype(BF16), wo_ref[half:, :]))
    else:
        x_ref, g_ref, wu_ref, wd_ref, gf_ref, o_ref, acc_ref = refs
        x = x_ref[...]
    h = (x * _rms_scale(x) * g_ref[...]).astype(BF16)
    acc_ref[...] = x

    def ff_chunk(c, carry):
        cols = pl.ds(pl.multiple_of(c * FF_CHUNK, FF_CHUNK), FF_CHUNK)
        a = jnp.maximum(_dot(h, wu_ref[:, cols]), 0.0)
        acc_ref[...] += _dot((a * a).astype(BF16), wd_ref[cols, :])
        return carry

    lax.fori_loop(0, wu_ref.shape[1] // FF_CHUNK, ff_chunk, 0)
    y = acc_ref[...]
    if with_final:
        y = y * _rms_scale(y) * gf_ref[...]
    o_ref[...] = y


def _mlp(x, g, wu_all, wd_all, gf, *, layer, row_tile, merge=None, with_final=False):
    n, d = x.shape
    row = lambda width: pl.BlockSpec((row_tile, width), lambda i: (i, 0))
    pick = lambda a, idx: pl.BlockSpec((None,) + a.shape[1:], lambda i: (idx,) + (0,) * (a.ndim - 1),
                                       pipeline_mode=pl.Buffered(1))
    args, specs = [x], [row(d)]
    if merge is not None:
        oa, ob, wo_all, e = merge
        args += [oa, ob, wo_all]
        specs += [row(oa.shape[1]), row(ob.shape[1]), pick(wo_all, e)]
    args += [g, wu_all, wd_all, gf]
    specs += [_const_spec(g.shape), pick(wu_all, layer), pick(wd_all, layer), _const_spec(gf.shape)]
    return pl.pallas_call(
        functools.partial(_mlp_kernel, with_merge=merge is not None, with_final=with_final),
        grid=(n // row_tile,), in_specs=specs, out_specs=row(d),
        out_shape=jax.ShapeDtypeStruct((n, d), F32),
        scratch_shapes=[pltpu.VMEM((row_tile, d), F32)],
        compiler_params=_params("arbitrary"), name="mlp",
    )(*args)


def kernel(x_prompt, x_sample, cache_fox_k, cache_fox_v, cache_fox_logf, state_gla, state_pool, page_table,
           norm_mix, norm_mlp, norm_final, w_in_even, w_gate_up, b_gate, gla_norm, b_forget, w_out_even,
           w_pool, pool_scale, w_mlp_up, w_mlp_down):
    n_prompt, seq, d = x_prompt.shape
    n_sample, t_new, _ = x_sample.shape
    depth = norm_mix.shape[0]
    n_even, n_pool = cache_fox_k.shape[0], cache_fox_k.shape[1]
    fox_dh = cache_fox_k.shape[-1]
    fox_total = FOX_HEADS * fox_dh
    dk, dv = state_gla.shape[3], state_gla.shape[4]
    dk_total, dv_total = GLA_HEADS * dk, GLA_HEADS * dv
    d_ff = w_mlp_up.shape[2]
    past_len = page_table.shape[1] * PAGE_SIZE
    T8 = SAMPLE_PAD

    xp = x_prompt.reshape(n_prompt * seq, d)
    xs = jnp.pad(x_sample, ((0, 0), (0, T8 - t_new), (0, 0))).reshape(n_sample * T8, d)

    row2 = lambda v: v.reshape(1, -1).astype(F32)
    wu = w_mlp_up.astype(BF16)
    wd = w_mlp_down.astype(BF16)
    wo = w_out_even.astype(BF16)
    w_in = _prep_in_proj(w_in_even, 2 * dk_total + dv_total)
    gfin = row2(norm_final)
    kv_p = kv_s = None
    cache_k = cache_fox_k.reshape(n_even, n_pool, PAGE_SIZE * FOX_HEADS, fox_dh)
    cache_v = cache_fox_v.reshape(n_even, n_pool, PAGE_SIZE * FOX_HEADS, fox_dh)
    cache_f = cache_fox_logf.transpose(0, 1, 3, 2).reshape(n_even, n_pool, FOX_HEADS * PAGE_SIZE)

    kp, vp, fp, ks, vs, fs, gp, gs, pp, ps = ([] for _ in range(10))
    for layer in range(depth):
        g_mix = row2(norm_mix[layer])
        g_mlp = row2(norm_mlp[layer])
        last = layer == depth - 1
        if layer % 2 == 0:
            e = layer // 2
            wg = jnp.zeros((LANES, dk_total), F32).at[EX_RANK_LANE:EX_RANK_LANE + GLA_RANK].set(
                w_gate_up[e]).astype(BF16)
            bg = row2(b_gate[e])
            bf = jnp.zeros((1, LANES), F32).at[0, EX_FORGET_LANE:EX_FORGET_LANE + FOX_HEADS].set(b_forget[e])
            gn = row2(gla_norm[e])

            qk, va, ga, qb, k_all, v_all, la, lf, fc, qaug, kaug = _project(
                xp, g_mix, w_in, wg, bg, bf, kv_p, e=e, n_even=n_even, seq_rows=seq, row_tile=ROW_TILE,
                act_dtype=BF16, with_aug=True)
            kv_p = (k_all, v_all)
            s0 = jnp.zeros((n_prompt, GLA_HEADS, dk, dv), F32)
            oa, sp_new = _gla(qk, la, va, ga, s0, gn, n_seq=n_prompt, rows=seq, real_rows=seq, out_dtype=BF16)
            ob = _fox_prompt(qb, qaug, k_all, v_all, kaug, e=e, n_seq=n_prompt, seq=seq, out_dtype=BF16)
            xp = _mlp(xp, g_mlp, wu, wd, gfin, layer=layer, row_tile=ROW_TILE, merge=(oa, ob, wo, e),
                      with_final=last)
            fp.append(lf[:, EX_FORGET_LANE:EX_FORGET_LANE + FOX_HEADS].reshape(n_prompt, seq, FOX_HEADS))
            gp.append(sp_new.astype(state_gla.dtype))

            qk, va, ga, qb, k_all, v_all, la, lf, fc = _project(
                xs, g_mix, w_in, wg, bg, bf, kv_s, e=e, n_even=n_even, seq_rows=T8, row_tile=n_sample * T8,
                act_dtype=F32, with_aug=False)
            kv_s = (k_all, v_all)
            oa, ss_new = _gla(qk, la, va, ga, state_gla[e].astype(F32), gn, n_seq=n_sample, rows=T8,
                              real_rows=t_new, out_dtype=F32)
            rh, cmask, bnew = _fox_bias(page_table, cache_f[e], fc, real_rows=t_new)
            ob = _fox_sample(page_table, qb, rh, cmask, bnew, k_all, v_all, cache_k, cache_v, layer=e)
            xs = _mlp(xs, g_mlp, wu, wd, gfin, layer=layer, row_tile=n_sample * T8, merge=(oa, ob, wo, e),
                      with_final=last)
            fs.append(lf[:, EX_FORGET_LANE:EX_FORGET_LANE + FOX_HEADS].reshape(n_sample, T8, FOX_HEADS)[:, :t_new])
            gs.append(ss_new.astype(state_gla.dtype))
        else:
            o = layer // 2
            wpl = w_pool[o].astype(BF16)
            sc = row2(pool_scale[o])
            halo = POOL_BUF + 1
            xp, tail = _pool(xp, g_mix, jnp.zeros((n_prompt, halo, d), F32), wpl, sc, n_seq=n_prompt,
                             rows=seq, row_tile=ROW_TILE, pos0=0)
            pp.append(tail[:, tail.shape[1] - POOL_BUF:])
            buf16 = jnp.pad(state_pool[o].astype(F32), ((0, 0), (1, 0), (0, 0)))
            xs, tail = _pool(xs, g_mix, buf16, wpl, sc, n_seq=n_sample, rows=T8, row_tile=T8, pos0=past_len)
            end = tail.shape[1] - (T8 - t_new)
            ps.append(tail[:, end - POOL_BUF:end])
            xp = _mlp(xp, g_mlp, wu, wd, gfin, layer=layer, row_tile=ROW_TILE, with_final=last)
            xs = _mlp(xs, g_mlp, wu, wd, gfin, layer=layer, row_tile=n_sample * T8, with_final=last)

    y_prompt = xp.reshape(n_prompt, seq, d)
    y_sample = xs.reshape(n_sample, T8, d)[:, :t_new]
    kp, vp = (a.reshape(n_even, n_prompt, seq, FOX_HEADS, fox_dh) for a in kv_p)
    ks, vs = (a.reshape(n_even, n_sample, T8, FOX_HEADS, fox_dh)[:, :, :t_new] for a in kv_s)
    return (y_prompt, y_sample, kp, vp, jnp.stack(fp), ks, vs, jnp.stack(fs), jnp.stack(gp), jnp.stack(gs),
            jnp.stack(pp), jnp.stack(ps))
```
